```python
import math
import jax, jax.numpy as jnp
from jax import lax
import numpy as np

D_MODEL = 1024
BATCH = 16
SEQ = 4096
DEPTH = 2
DEC_BATCH = 1
DEC_SEQ = 16384
PAST_LEN = 128

HEAD_DIM = 64
H_NA = 4
H_DIL = 6
H_GDN = 6
W_NA = H_NA * HEAD_DIM
W_DIL = H_DIL * HEAD_DIM
W_GDN = H_GDN * HEAD_DIM
GRID_W = 64
NA_KH = 8
NA_KW = 16
NA_QB = 16
NA_BAND = 32
DIL_PATTERN = ((128, 1), (512, 4), (2048, 16))
DIL_HPG = H_DIL // len(DIL_PATTERN)
BAND_Q = 64
ROPE_THETA = 10000.0
GDN_CHUNK = 64
CONV_K = 5
N_EXPERTS = 16
D_EXPERT = 2048
EC_FACTOR = 2
N_BRANCH = 3
D_IN = 3 * W_NA + 3 * W_DIL + 4 * W_GDN + 4 * H_GDN + N_BRANCH * D_MODEL
DN_ALPHA = (2 * DEPTH) ** 0.25
DN_BETA = (8 * DEPTH) ** -0.25
LN_EPS = 1e-5
NORM_EPS = 1e-6

kernel_name = 'hybrid_natten_longnet_gdn_ec_encoder'


def layer_norm(x, g, b):
    xf = x.astype(jnp.float32)
    mu = jnp.mean(xf, axis=-1, keepdims=True)
    var = jnp.mean(jnp.square(xf - mu), axis=-1, keepdims=True)
    return ((xf - mu) * lax.rsqrt(var + LN_EPS) * g + b).astype(x.dtype)


def rope(x, pos):
    half = x.shape[-1] // 2
    inv = ROPE_THETA ** (-jnp.arange(half, dtype=jnp.float32) / half)
    ang = pos.astype(jnp.float32)[:, None] * inv[None, :]
    cos, sin = jnp.cos(ang)[:, None, :], jnp.sin(ang)[:, None, :]
    x1, x2 = x[..., :half], x[..., half:]
    return jnp.concatenate([x1 * cos - x2 * sin, x1 * sin + x2 * cos], axis=-1)


def neighbourhood_attention(q, k, v, rpb):
    b, s, h, dh = q.shape
    rows = s // GRID_W
    kh = min(NA_KH, rows)
    n_cb = GRID_W // NA_QB
    r = jnp.arange(rows)
    row_idx = jnp.clip(r - kh // 2, 0, rows - kh)[:, None] + jnp.arange(kh)[None, :]
    band_start = jnp.clip(jnp.arange(n_cb) * NA_QB - (NA_BAND - NA_QB) // 2, 0, GRID_W - NA_BAND)
    col_idx = band_start[:, None] + jnp.arange(NA_BAND)[None, :]
    qc = jnp.arange(n_cb)[:, None] * NA_QB + jnp.arange(NA_QB)[None, :]
    win_start = jnp.clip(qc - NA_KW // 2, 0, GRID_W - NA_KW)
    kcol = col_idx[:, None, :]
    col_ok = (kcol >= win_start[:, :, None]) & (kcol < win_start[:, :, None] + NA_KW)
    dr = row_idx - r[:, None] + (NA_KH - 1)
    dc = jnp.clip(kcol - qc[:, :, None], -(NA_KW - 1), NA_KW - 1) + (NA_KW - 1)
    bias = rpb[:, dr[:, None, None, :, None], dc[None, :, :, None, :]]
    qg = q.astype(jnp.float32).reshape(b, rows, n_cb, NA_QB, h, dh) * dh ** -0.5
    kg = k.astype(jnp.float32).reshape(b, rows, GRID_W, h, dh)
    vg = v.astype(jnp.float32).reshape(b, rows, GRID_W, h, dh)
    gi_r, gi_c = row_idx[:, None, :, None], col_idx[None, :, None, :]
    kn = kg[:, gi_r, gi_c]
    vn = vg[:, gi_r, gi_c]
    sc = jnp.einsum('brjqhd,brjiwhd->bhrjqiw', qg, kn) + bias
    sc = jnp.where(col_ok[:, :, None, :], sc, -jnp.inf)
    p = jax.nn.softmax(sc, axis=(-2, -1))
    o = jnp.einsum('bhrjqiw,brjiwhd->brjqhd', p, vn)
    return o.reshape(b, s, h * dh)


def banded_attention(q, k, v, half):
    n, L, h, dh = q.shape
    qb = min(BAND_Q, L)
    nb = -(-L // qb)
    Lp = nb * qb
    width = qb + 2 * half
    qp = jnp.pad(q.astype(jnp.float32), ((0, 0), (0, Lp - L), (0, 0), (0, 0))).reshape(n, nb, qb, h, dh)
    pad_kv = ((0, 0), (half, Lp - L + half), (0, 0), (0, 0))
    kp = jnp.pad(k.astype(jnp.float32), pad_kv)
    vp = jnp.pad(v.astype(jnp.float32), pad_kv)
    starts = jnp.arange(nb) * qb
    kidx = starts[:, None] + jnp.arange(width)[None, :]
    kb = kp[:, kidx]
    vb = vp[:, kidx]
    qpos = starts[:, None] + jnp.arange(qb)[None, :]
    kpos = (kidx - half)[:, None, :]
    valid = (jnp.abs(qpos[:, :, None] - kpos) <= half) & (kpos >= 0) & (kpos < L)
    sc = jnp.einsum('nbqhd,nbkhd->nhbqk', qp, kb) * dh ** -0.5
    sc = jnp.where(valid, sc, -jnp.inf)
    m = jnp.max(sc, axis=-1, keepdims=True)
    p = jnp.exp(sc - m)
    den = jnp.sum(p, axis=-1)
    den_t = jnp.moveaxis(den, 1, -1)
    o = jnp.einsum('nhbqk,nbkhd->nbqhd', p, vb) / den_t[..., None]
    o = o.reshape(n, Lp, h, dh)[:, :L]
    m_t = jnp.moveaxis(m[..., 0], 1, -1).reshape(n, Lp, h)[:, :L]
    return o, m_t, den_t.reshape(n, Lp, h)[:, :L]


def dilated_attention(q, k, v):
    b, s, h, dh = q.shape
    outs, maxes, dens = [], [], []
    for g, (win, dil) in enumerate(DIL_PATTERN):
        lo, hi = g * DIL_HPG, (g + 1) * DIL_HPG
        half = win // (2 * dil)
        L = s // dil

        def to_classes(t):
            t = t[:, :, lo:hi].reshape(b, L, dil, DIL_HPG, dh)
            return jnp.swapaxes(t, 1, 2).reshape(b * dil, L, DIL_HPG, dh)

        def from_classes(t):
            rest = t.shape[2:]
            return jnp.swapaxes(t.reshape(b, dil, L, *rest), 1, 2).reshape(b, s, *rest)

        o, m, den = banded_attention(to_classes(q), to_classes(k), to_classes(v), half)
        outs.append(from_classes(o))
        maxes.append(from_classes(m))
        dens.append(from_classes(den))
    o = jnp.stack(outs)
    m = jnp.stack(maxes)
    den = jnp.stack(dens)
    wgt = den * jnp.exp(m - jnp.max(m, axis=0))
    o = jnp.sum(wgt[..., None] * o, axis=0) / jnp.sum(wgt, axis=0)[..., None]
    return o.reshape(b, s, DIL_HPG * dh)


def centred_depthwise_conv(x, w):
    c = x.shape[-1]
    return lax.conv_general_dilated(x, w[:, None, :].astype(x.dtype), window_strides=(1,),
                                    padding=[(CONV_K // 2, CONV_K // 2)],
                                    dimension_numbers=('NWC', 'WIO', 'NWC'), feature_group_count=c)


def l2_normalize(x):
    return x * lax.rsqrt(jnp.sum(jnp.square(x), axis=-1, keepdims=True) + NORM_EPS)


def gated_delta_chunked(q, k, v, beta, g):
    b, s, h, dk = q.shape
    c = GDN_CHUNK
    n = s // c

    def chunks(t):
        return jnp.moveaxis(t.reshape(b, n, c, h, -1), 3, 1)

    q = chunks(q) * dk ** -0.5
    k = chunks(k)
    v = chunks(v)
    beta = jnp.moveaxis(beta.reshape(b, n, c, h), 3, 1)
    gc = jnp.cumsum(jnp.moveaxis(g.reshape(b, n, c, h), 3, 1), axis=-1)
    incl = jnp.tril(jnp.ones((c, c), bool))
    strict = jnp.tril(jnp.ones((c, c), bool), -1)
    diff = gc[..., :, None] - gc[..., None, :]
    decay = jnp.where(incl, jnp.exp(jnp.where(incl, diff, 0.0)), 0.0)
    kb = k * beta[..., None]
    a_mat = jnp.where(strict, jnp.einsum('bhnid,bhnjd->bhnij', kb, k) * decay, 0.0)
    eye = jnp.eye(c, dtype=a_mat.dtype)
    t_mat = lax.linalg.triangular_solve(a_mat + eye, jnp.broadcast_to(eye, a_mat.shape),
                                        left_side=True, lower=True, unit_diagonal=True)
    w = jnp.einsum('bhnij,bhnjd->bhnid', t_mat, kb * jnp.exp(gc)[..., None])
    u = jnp.einsum('bhnij,bhnjd->bhnid', t_mat, v * beta[..., None])
    intra = jnp.where(incl, jnp.einsum('bhnid,bhnjd->bhnij', q, k) * decay, 0.0)
    q_dec = q * jnp.exp(gc)[..., None]
    k_dec = k * jnp.exp(gc[..., -1:] - gc)[..., None]
    g_last = jnp.exp(gc[..., -1])

    def step(state, inp):
        q_i, k_i, u_i, w_i, intra_i, gl_i = inp
        v_new = u_i - jnp.einsum('bhcd,bhde->bhce', w_i, state)
        o_i = jnp.einsum('bhcd,bhde->bhce', q_i, state) + jnp.einsum('bhij,bhje->bhie', intra_i, v_new)
        state = state * gl_i[..., None, None] + jnp.einsum('bhcd,bhce->bhde', k_i, v_new)
        return state, o_i

    xs = tuple(jnp.moveaxis(t, 2, 0) for t in (q_dec, k_dec, u, w, intra, g_last))
    state0 = jnp.zeros((b, h, dk, v.shape[-1]), q.dtype)
    _, o = lax.scan(step, state0, xs)
    o = jnp.moveaxis(jnp.moveaxis(o, 0, 2), 1, 3)
    return o.reshape(b, s, h, -1)


def gdn_branch(q, k, v, z, b_logit, a_logit, conv_w, a_log, dt_bias, norm_w):
    b, s, _ = q.shape
    qkv = jax.nn.silu(centred_depthwise_conv(jnp.concatenate([q, k, v], axis=-1), conv_w)).astype(jnp.float32)
    q, k, v = jnp.split(qkv, 3, axis=-1)
    q = l2_normalize(q.reshape(b, s, H_GDN, HEAD_DIM))
    k = l2_normalize(k.reshape(b, s, H_GDN, HEAD_DIM))
    v = v.reshape(b, s, H_GDN, HEAD_DIM)
    beta = jax.nn.sigmoid(b_logit.astype(jnp.float32)).reshape(b, s, 2, H_GDN)
    g = -jnp.exp(a_log) * jax.nn.softplus(a_logit.astype(jnp.float32).reshape(b, s, 2, H_GDN) + dt_bias)
    o_fwd = gated_delta_chunked(q, k, v, beta[:, :, 0], g[:, :, 0])
    rev = lambda t: jnp.flip(t, axis=1)
    o_bwd = rev(gated_delta_chunked(rev(q), rev(k), rev(v), rev(beta[:, :, 1]), rev(g[:, :, 1])))
    o = o_fwd + o_bwd
    o = o * lax.rsqrt(jnp.mean(jnp.square(o), axis=-1, keepdims=True) + NORM_EPS) * norm_w
    return o.reshape(b, s, W_GDN) * jax.nn.silu(z.astype(jnp.float32))


def token_mixer(x, w_in, na_rpb, conv_w, a_log, dt_bias, gdn_norm_w, w_br_na, w_br_dil, w_br_gdn, w_out):
    b, s, _ = x.shape
    proj = jnp.einsum('bsd,de->bse', x, w_in)
    sizes = (W_NA,) * 3 + (W_DIL,) * 3 + (W_GDN,) * 4 + (2 * H_GDN, 2 * H_GDN, N_BRANCH * D_MODEL)
    offsets = [int(o) for o in np.cumsum(sizes)[:-1]]
    qa, ka, va, qd, kd, vd, qc, kc, vc, zc, bc, ac, gates = jnp.split(proj, offsets, axis=-1)
    heads = lambda t, h: t.reshape(b, s, h, HEAD_DIM)
    o_na = neighbourhood_attention(heads(qa, H_NA), heads(ka, H_NA), heads(va, H_NA), na_rpb)
    pos = jnp.arange(s)
    o_dil = dilated_attention(rope(heads(qd, H_DIL).astype(jnp.float32), pos),
                              rope(heads(kd, H_DIL).astype(jnp.float32), pos), heads(vd, H_DIL))
    o_gdn = gdn_branch(qc, kc, vc, zc, bc, ac, conv_w, a_log, dt_bias, gdn_norm_w)
    gate = jax.nn.sigmoid(gates.astype(jnp.float32)).reshape(b, s, N_BRANCH, D_MODEL)
    br_na = jnp.einsum('bsw,wd->bsd', o_na.astype(x.dtype), w_br_na)
    br_dil = jnp.einsum('bsw,wd->bsd', o_dil.astype(x.dtype), w_br_dil)
    br_gdn = jnp.einsum('bsw,wd->bsd', o_gdn.astype(x.dtype), w_br_gdn)
    merged = gate[:, :, 0] * br_na + gate[:, :, 1] * br_dil + gate[:, :, 2] * br_gdn
    return jnp.einsum('bsd,de->bse', merged.astype(x.dtype), w_out)


def expert_choice_ffn(x, w_router, w_up, w_gate, w_down):
    b, s, d = x.shape
    n = b * s
    t = x.reshape(n, d)
    aff = jax.nn.softmax(jnp.einsum('nd,de->ne', t, w_router).astype(jnp.float32), axis=-1)
    cap = EC_FACTOR * n // N_EXPERTS
    gate, idx = lax.top_k(aff.T, cap)
    xe = t[idx]
    hid = jax.nn.silu(jnp.einsum('ecd,edf->ecf', xe, w_gate)) * jnp.einsum('ecd,edf->ecf', xe, w_up)
    ye = jnp.einsum('ecf,efd->ecd', hid, w_down) * gate[..., None].astype(x.dtype)
    y = jnp.zeros_like(t).at[idx.reshape(-1)].add(ye.reshape(-1, d))
    return y.reshape(b, s, d)


def trunk(x, w_in, na_rpb, conv_w, a_log, dt_bias, gdn_norm_w, w_br_na, w_br_dil, w_br_gdn, w_out,
          ln1_g, ln1_b, w_router, w_up, w_gate, w_down, ln2_g, ln2_b):
    for l in range(DEPTH):
        mix = token_mixer(x, w_in[l], na_rpb[l], conv_w[l], a_log[l], dt_bias[l], gdn_norm_w[l],
                          w_br_na[l], w_br_dil[l], w_br_gdn[l], w_out[l])
        x = layer_norm(DN_ALPHA * x + mix, ln1_g[l], ln1_b[l])
        ffn = expert_choice_ffn(x, w_router[l], w_up[l], w_gate[l], w_down[l])
        x = layer_norm(DN_ALPHA * x + ffn, ln2_g[l], ln2_b[l])
    return x


def setup_inputs(seed: int = 0) -> dict:
    key = jax.random.key(seed)
    ks = jax.random.split(key, 20)
    f32 = jnp.float32
    nrm = lambda k, shape, scale: jax.random.normal(k, shape, f32) * scale
    w_dil_out = DIL_HPG * HEAD_DIM
    dt = jnp.exp(jax.random.uniform(ks[6], (DEPTH, 2, H_GDN), f32, math.log(1e-3), math.log(1e-1)))
    return {
        'x_prompt': nrm(ks[0], (BATCH, SEQ, D_MODEL), 1.0),
        'x_sample': nrm(ks[1], (DEC_BATCH, DEC_SEQ, D_MODEL), 1.0),
        'w_in': nrm(ks[2], (DEPTH, D_MODEL, D_IN), D_MODEL ** -0.5),
        'na_rpb': nrm(ks[3], (DEPTH, H_NA, 2 * NA_KH - 1, 2 * NA_KW - 1), 0.1),
        'conv_w': nrm(ks[4], (DEPTH, CONV_K, 3 * W_GDN), CONV_K ** -0.5),
        'a_log': jnp.log(jax.random.uniform(ks[5], (DEPTH, 2, H_GDN), f32, 1.0, 16.0)),
        'dt_bias': dt + jnp.log(-jnp.expm1(-dt)),
        'gdn_norm_w': 1.0 + nrm(ks[7], (DEPTH, HEAD_DIM), 0.01),
        'w_br_na': nrm(ks[8], (DEPTH, W_NA, D_MODEL), W_NA ** -0.5 * DN_BETA),
        'w_br_dil': nrm(ks[9], (DEPTH, w_dil_out, D_MODEL), w_dil_out ** -0.5 * DN_BETA),
        'w_br_gdn': nrm(ks[10], (DEPTH, W_GDN, D_MODEL), W_GDN ** -0.5 * DN_BETA),
        'w_out': nrm(ks[11], (DEPTH, D_MODEL, D_MODEL), D_MODEL ** -0.5 * DN_BETA),
        'ln1_g': 1.0 + nrm(ks[12], (DEPTH, D_MODEL), 0.01),
        'ln1_b': nrm(ks[13], (DEPTH, D_MODEL), 0.01),
        'w_router': nrm(ks[14], (DEPTH, D_MODEL, N_EXPERTS), D_MODEL ** -0.5),
        'w_up': nrm(ks[15], (DEPTH, N_EXPERTS, D_MODEL, D_EXPERT), D_MODEL ** -0.5),
        'w_gate': nrm(ks[16], (DEPTH, N_EXPERTS, D_MODEL, D_EXPERT), D_MODEL ** -0.5),
        'w_down': nrm(ks[17], (DEPTH, N_EXPERTS, D_EXPERT, D_MODEL), D_EXPERT ** -0.5 * DN_BETA),
        'ln2_g': 1.0 + nrm(ks[18], (DEPTH, D_MODEL), 0.01),
        'ln2_b': nrm(ks[19], (DEPTH, D_MODEL), 0.01),
    }


def reference(x_prompt, x_sample, w_in, na_rpb, conv_w, a_log, dt_bias, gdn_norm_w, w_br_na, w_br_dil,
              w_br_gdn, w_out, ln1_g, ln1_b, w_router, w_up, w_gate, w_down, ln2_g, ln2_b):
    y_prompt = trunk(x_prompt, w_in, na_rpb, conv_w, a_log, dt_bias, gdn_norm_w, w_br_na, w_br_dil, w_br_gdn,
                     w_out, ln1_g, ln1_b, w_router, w_up, w_gate, w_down, ln2_g, ln2_b)
    y_sample = trunk(x_sample, w_in, na_rpb, conv_w, a_log, dt_bias, gdn_norm_w, w_br_na, w_br_dil, w_br_gdn,
                     w_out, ln1_g, ln1_b, w_router, w_up, w_gate, w_down, ln2_g, ln2_b)
    return (y_prompt, y_sample)
```

```python
import functools
import math

import numpy as np
import jax
import jax.numpy as jnp
from jax import lax
from jax.experimental import pallas as pl
from jax.experimental.pallas import tpu as pltpu

F32 = jnp.float32
BF16 = jnp.bfloat16
I32 = jnp.int32

D_MODEL = 1024
DEPTH = 2
HEAD_DIM = 64
H_NA = 4
H_DIL = 6
H_GDN = 6
W_NA = H_NA * HEAD_DIM
W_DIL = H_DIL * HEAD_DIM
W_GDN = H_GDN * HEAD_DIM
GRID_W = 64
NA_KH = 8
NA_KW = 16
DIL_PATTERN = ((128, 1), (512, 4), (2048, 16))
DIL_HPG = H_DIL // len(DIL_PATTERN)
W_DIL_OUT = DIL_HPG * HEAD_DIM
ROPE_THETA = 10000.0
GDN_CHUNK = 64
CONV_K = 5
N_EXPERTS = 16
D_EXPERT = 2048
EC_FACTOR = 2
N_BRANCH = 3
DN_ALPHA = (2 * DEPTH) ** 0.25
LN_EPS = 1e-5
NORM_EPS = 1e-6
NEG_BIG = -1e30

C_GDN = 0
C_DIL = C_GDN + 3 * W_GDN
C_NA = C_DIL + 3 * W_DIL
C_GATE = C_NA + 3 * W_NA
C_Z = C_GATE + N_BRANCH * D_MODEL
C_BA = C_Z + W_GDN
LANE = 128
D_PAD = C_BA + LANE
VMEM_LIMIT = 56 * 1024 * 1024


def _cparams(sem):
    return pltpu.CompilerParams(dimension_semantics=sem, vmem_limit_bytes=VMEM_LIMIT)


def _dot(a, b, dims=None, hi=False):
    if dims is None:
        dims = (((a.ndim - 1,), (0,)), ((), ()))
    if hi:
        return lax.dot_general(a.astype(F32), b.astype(F32), dims, precision=lax.Precision.HIGHEST,
                               preferred_element_type=F32)
    return lax.dot_general(a.astype(BF16), b.astype(BF16), dims, preferred_element_type=F32)


NT = (((1,), (1,)), ((), ()))
TN = (((0,), (0,)), ((), ()))


def _sigmoid(x):
    return 1.0 / (1.0 + jnp.exp(-x))


def _inproj_kernel(x_ref, w_ref, o_ref, xb_ref):
    @pl.when(pl.program_id(1) == 0)
    def _():
        xb_ref[...] = x_ref[...].astype(BF16)

    o_ref[...] = jnp.dot(xb_ref[...], w_ref[...], preferred_element_type=F32)


def _inproj(x, w_bf16, tm=1024, tn=512):
    n, k = x.shape
    npad = w_bf16.shape[1]
    tm = min(tm, n)
    return pl.pallas_call(
        _inproj_kernel,
        out_shape=jax.ShapeDtypeStruct((n, npad), F32),
        grid=(n // tm, npad // tn),
        in_specs=[pl.BlockSpec((tm, k), lambda i, j: (i, 0)),
                  pl.BlockSpec((k, tn), lambda i, j: (0, j))],
        out_specs=pl.BlockSpec((tm, tn), lambda i, j: (i, j)),
        scratch_shapes=[pltpu.VMEM((tm, k), BF16)],
        compiler_params=_cparams(("parallel", "arbitrary")),
        name="inproj",
    )(x, w_bf16)


def _reorder_w_in(w):
    o_qd = 3 * W_NA
    o_qc = o_qd + 3 * W_DIL
    o_z = o_qc + 3 * W_GDN
    o_b = o_z + W_GDN
    o_gate = o_b + 4 * H_GDN
    parts = [w[:, o_qc:o_z], w[:, o_qd:o_qc], w[:, 0:o_qd], w[:, o_gate:], w[:, o_z:o_b], w[:, o_b:o_gate],
             jnp.zeros((w.shape[0], LANE - 4 * H_GDN), w.dtype)]
    return jnp.concatenate(parts, axis=1).astype(BF16)


NA_RQ = 8
NA_WIN_ROWS = 16
NA_KB = 256


def _na_bias_table(rpb):
    off = np.arange(NA_KH)[:, None, None, None]
    i = np.arange(NA_KH)[None, None, :, None]
    qc = np.arange(GRID_W)[None, :, None, None]
    kc = np.arange(GRID_W)[None, None, None, :]
    dr = i - off + (NA_KH - 1)
    dc = np.clip(kc - qc, -(NA_KW - 1), NA_KW - 1) + (NA_KW - 1)
    ws = np.clip(qc - NA_KW // 2, 0, GRID_W - NA_KW)
    ok = (kc >= ws) & (kc < ws + NA_KW)
    dr, dc, ok = np.broadcast_arrays(dr, dc, ok)
    bias = rpb[:, dr, dc]
    bias = jnp.where(jnp.asarray(ok)[None], bias, NEG_BIG)
    bias = jnp.transpose(bias, (1, 0, 2, 3, 4))
    return bias.reshape(NA_KH, H_NA * GRID_W, NA_KH * GRID_W).astype(F32)


def _na_kernel(rows, q_ref, k0, k1, k2, k3, v0, v1, v2, v3, bias_ref, o_ref, kc_ref, vc_ref):
    rq = pl.program_id(1)
    for j, (kr, vr) in enumerate(((k0, v0), (k1, v1), (k2, v2), (k3, v3))):
        kc_ref[j * NA_KB:(j + 1) * NA_KB, :] = kr[...].astype(BF16)
        vc_ref[j * NA_KB:(j + 1) * NA_KB, :] = vr[...].astype(BF16)
    ws = jnp.clip(rq * NA_RQ - NA_KH // 2, 0, rows - NA_WIN_ROWS)
    lane = lax.broadcasted_iota(I32, (1, W_NA), 1)
    masks = [(lane // HEAD_DIM == h).astype(F32) for h in range(H_NA)]
    for j in range(NA_RQ):
        r = rq * NA_RQ + j
        r0 = jnp.clip(r - NA_KH // 2, 0, rows - NA_KH)
        koff = pl.multiple_of((r0 - ws) * GRID_W, GRID_W)
        off = r - r0
        qj = q_ref[j * GRID_W:(j + 1) * GRID_W, :] * (HEAD_DIM ** -0.5)
        qs = jnp.concatenate([qj * m for m in masks], axis=0).astype(BF16)
        kw = kc_ref[pl.ds(koff, NA_KH * GRID_W), :]
        vw = vc_ref[pl.ds(koff, NA_KH * GRID_W), :]
        s = _dot(qs, kw, NT) + bias_ref[off]
        m = jnp.max(s, axis=-1, keepdims=True)
        p = jnp.exp(s - m)
        den = jnp.sum(p, axis=-1, keepdims=True)
        o = _dot(p, vw) / den
        acc = o[0:GRID_W] * masks[0]
        for h in range(1, H_NA):
            acc = acc + o[h * GRID_W:(h + 1) * GRID_W] * masks[h]
        o_ref[j * GRID_W:(j + 1) * GRID_W, :] = acc


def _na_attention(proj3, bias_tab):
    b, s, _ = proj3.shape
    rows = s // GRID_W
    nrq = rows // NA_RQ
    tq = NA_RQ * GRID_W
    cq, ck, cv = C_NA // W_NA, C_NA // W_NA + 1, C_NA // W_NA + 2
    nkb = s // NA_KB

    def kmap(j, col):
        def f(bi, rq):
            kb0 = jnp.clip(2 * rq - 1, 0, nkb - NA_WIN_ROWS * GRID_W // NA_KB)
            return (bi, kb0 + j, col)
        return f

    in_specs = [pl.BlockSpec((None, tq, W_NA), lambda bi, rq: (bi, rq, cq))]
    in_specs += [pl.BlockSpec((None, NA_KB, W_NA), kmap(j, ck)) for j in range(4)]
    in_specs += [pl.BlockSpec((None, NA_KB, W_NA), kmap(j, cv)) for j in range(4)]
    in_specs += [pl.BlockSpec(bias_tab.shape, lambda bi, rq: (0, 0, 0))]
    return pl.pallas_call(
        functools.partial(_na_kernel, rows),
        out_shape=jax.ShapeDtypeStruct((b, s, W_NA), F32),
        grid=(b, nrq),
        in_specs=in_specs,
        out_specs=pl.BlockSpec((None, tq, W_NA), lambda bi, rq: (bi, rq, 0)),
        scratch_shapes=[pltpu.VMEM((NA_WIN_ROWS * GRID_W, W_NA), BF16),
                        pltpu.VMEM((NA_WIN_ROWS * GRID_W, W_NA), BF16)],
        compiler_params=_cparams(("parallel", "parallel")),
        name="na_attention",
    )(proj3, *([proj3] * 8), bias_tab)


def _rope_tables(s):
    half = HEAD_DIM // 2
    inv = ROPE_THETA ** (-jnp.arange(half, dtype=F32) / half)
    ang = jnp.arange(s, dtype=F32)[:, None] * inv[None, :]
    cos, sin = jnp.cos(ang), jnp.sin(ang)
    cos_t = jnp.concatenate([cos, cos, cos, cos], axis=1)
    sin_t = jnp.concatenate([-sin, sin, -sin, sin], axis=1)
    return cos_t, sin_t


def _rope_kernel(q_ref, k_ref, v_ref, cos_ref, sin_ref, qo_ref, ko_ref, vo_ref):
    cos = jnp.concatenate([cos_ref[...]] * 3, axis=1)
    sin = jnp.concatenate([sin_ref[...]] * 3, axis=1)
    lane = lax.broadcasted_iota(I32, (1, W_DIL), 1)
    first = (lane % HEAD_DIM) < (HEAD_DIM // 2)

    def rot(x):
        up = pltpu.roll(x, W_DIL - HEAD_DIM // 2, 1)
        dn = pltpu.roll(x, HEAD_DIM // 2, 1)
        return x * cos + jnp.where(first, up, dn) * sin

    qo_ref[...] = (rot(q_ref[...]) * (HEAD_DIM ** -0.5)).astype(BF16)
    ko_ref[...] = rot(k_ref[...]).astype(BF16)
    vo_ref[...] = v_ref[...].astype(BF16)


def _rope(proj, s, cos_t, sin_t, tt=512):
    n = proj.shape[0]
    tt = min(tt, s)
    nps = s // tt
    c0 = C_DIL // W_DIL
    spec = lambda c: pl.BlockSpec((tt, W_DIL), lambda i: (i, c))
    tspec = pl.BlockSpec((tt, LANE), lambda i: (i % nps, 0))
    out = jax.ShapeDtypeStruct((n, W_DIL), BF16)
    return pl.pallas_call(
        _rope_kernel,
        out_shape=(out, out, out),
        grid=(n // tt,),
        in_specs=[spec(c0), spec(c0 + 1), spec(c0 + 2), tspec, tspec],
        out_specs=(pl.BlockSpec((tt, W_DIL), lambda i: (i, 0)),) * 3,
        compiler_params=_cparams(("parallel",)),
        name="rope",
    )(proj, proj, proj, cos_t, sin_t)


DIL_QB = 256


def _dil_kernel(length, half, q_ref, kp_ref, kc_ref, kn_ref, vp_ref, vc_ref, vn_ref, num_ref, m_ref, l_ref):
    i = pl.program_id(2)
    qb = q_ref.shape[0]
    lane = lax.broadcasted_iota(I32, (1, LANE), 1)
    m0 = (lane < HEAD_DIM)
    q = q_ref[...]
    zero = jnp.zeros_like(q)
    qs = jnp.concatenate([jnp.where(m0, q, zero), jnp.where(m0, zero, q)], axis=0)
    kcat = jnp.concatenate([kp_ref[...], kc_ref[...], kn_ref[...]], axis=0)
    vcat = jnp.concatenate([vp_ref[...], vc_ref[...], vn_ref[...]], axis=0)
    s = _dot(qs, kcat, NT)
    qpos = i * qb + lax.broadcasted_iota(I32, (2 * qb, 1), 0) % qb
    kpos = (i - 1) * qb + lax.broadcasted_iota(I32, (1, 3 * qb), 1)
    valid = (jnp.abs(qpos - kpos) <= half) & (kpos >= 0) & (kpos < length)
    s = jnp.where(valid, s, NEG_BIG)
    m = jnp.max(s, axis=-1, keepdims=True)
    p = jnp.where(valid, jnp.exp(s - m), 0.0)
    den = jnp.sum(p, axis=-1, keepdims=True)
    o = _dot(p, vcat)
    num_ref[...] = jnp.where(m0, o[:qb], o[qb:])
    m_ref[...] = jnp.where(m0, m[:qb], m[qb:])
    l_ref[...] = jnp.where(m0, den[:qb], den[qb:])


def _dil_group(qr, kr, vb, b, s, g):
    win, dil = DIL_PATTERN[g]
    half = win // (2 * dil)
    length = s // dil
    qb = min(DIL_QB, length)
    nb = length // qb
    ncol = W_DIL // LANE
    view = lambda t: t.reshape(b, length, dil * W_DIL)
    qv, kv, vv = view(qr), view(kr), view(vb)

    def spec(shift):
        def f(bi, c, i):
            return (bi, jnp.clip(i + shift, 0, nb - 1), c * ncol + g)
        return pl.BlockSpec((None, qb, LANE), f)

    out = jax.ShapeDtypeStruct((b, length, dil * LANE), F32)
    ospec = pl.BlockSpec((None, qb, LANE), lambda bi, c, i: (bi, i, c))
    num, m, l = pl.pallas_call(
        functools.partial(_dil_kernel, length, half),
        out_shape=(out, out, out),
        grid=(b, dil, nb),
        in_specs=[spec(0), spec(-1), spec(0), spec(1), spec(-1), spec(0), spec(1)],
        out_specs=(ospec, ospec, ospec),
        compiler_params=_cparams(("parallel", "parallel", "parallel")),
        name=f"dilated_attention_g{g}",
    )(qv, kv, kv, kv, vv, vv, vv)
    flat = lambda t: t.reshape(b * s, LANE)
    return flat(num), flat(m), flat(l)


GDN_TT = 512
N_CHAIN = 2 * H_GDN
W_CHAIN = N_CHAIN * HEAD_DIM
CHAIN_TILE = 256


def _head_ones(width):
    idx = np.arange(width) // HEAD_DIM
    return jnp.asarray((idx[:, None] == idx[None, :]).astype(np.float32))


def _gdn_prep_kernel(seq_tiles, x_ref, xp_ref, xn_ref, ba_ref, cw_ref, alog_ref, dtb_ref, ones_ref, eb_ref, tri_ref,
                     q_ref, k_ref, v_ref, beta_ref, gc_ref):
    i = pl.program_id(1)
    x = x_ref[...]
    tt = x.shape[0]
    row = lax.broadcasted_iota(I32, (tt, 1), 0)
    prev = jnp.where(i > 0, xp_ref[...], 0.0)
    nxt = jnp.where(i < seq_tiles - 1, xn_ref[...], 0.0)
    acc = x * cw_ref[CONV_K // 2:CONV_K // 2 + 1, :]
    for d in (1, 2):
        dn = pltpu.roll(x, d, 0)
        for e in range(d):
            dn = jnp.where(row == e, prev[8 - d + e:8 - d + e + 1, :], dn)
        acc = acc + dn * cw_ref[CONV_K // 2 - d:CONV_K // 2 - d + 1, :]
        up = pltpu.roll(x, tt - d, 0)
        for e in range(d):
            up = jnp.where(row == tt - d + e, nxt[e:e + 1, :], up)
        acc = acc + up * cw_ref[CONV_K // 2 + d:CONV_K // 2 + d + 1, :]
    act = acc * _sigmoid(acc)
    q, k, v = act[:, :W_GDN], act[:, W_GDN:2 * W_GDN], act[:, 2 * W_GDN:]
    ones = ones_ref[...]
    q_ref[...] = q * lax.rsqrt(_dot(q * q, ones, hi=True) + NORM_EPS) * (HEAD_DIM ** -0.5)
    k_ref[...] = k * lax.rsqrt(_dot(k * k, ones, hi=True) + NORM_EPS)
    v_ref[...] = v
    ba = ba_ref[...]
    beta = _sigmoid(ba)
    z = ba + dtb_ref[...]
    g = -jnp.exp(alog_ref[...]) * (jnp.maximum(z, 0.0) + jnp.log1p(jnp.exp(-jnp.abs(z))))
    lane = lax.broadcasted_iota(I32, (1, LANE), 1)
    bg = jnp.where(lane < N_CHAIN, beta, g)
    expd = _dot(bg, eb_ref[...], hi=True)
    beta_ref[...] = expd[:, :W_CHAIN]
    gx = expd[:, W_CHAIN:]
    gc_f = _dot(tri_ref[0], gx[:, :W_GDN], hi=True)
    gc_b = _dot(tri_ref[1], gx[:, W_GDN:], hi=True)
    gc_ref[...] = jnp.concatenate([gc_f, gc_b], axis=1)


def _gdn_prep(proj3, conv_w, a_log, dt_bias):
    b, s, _ = proj3.shape
    tt = min(GDN_TT, s)
    nt = s // tt
    w3 = 3 * W_GDN
    cw = jnp.zeros((8, w3), F32).at[:CONV_K].set(conv_w)
    alog_row = jnp.zeros((1, LANE), F32).at[0, N_CHAIN:2 * N_CHAIN].set(a_log.reshape(-1))
    dtb_row = jnp.zeros((1, LANE), F32).at[0, N_CHAIN:2 * N_CHAIN].set(dt_bias.reshape(-1))
    eb = np.zeros((LANE, 2 * W_CHAIN), np.float32)
    for c in range(N_CHAIN):
        eb[c, c * HEAD_DIM:(c + 1) * HEAD_DIM] = 1.0
        eb[N_CHAIN + c, W_CHAIN + c * HEAD_DIM:W_CHAIN + (c + 1) * HEAD_DIM] = 1.0
    t = np.arange(tt)
    same = (t[:, None] // GDN_CHUNK) == (t[None, :] // GDN_CHUNK)
    tri = np.stack([same & (t[None, :] <= t[:, None]), same & (t[None, :] >= t[:, None])]).astype(np.float32)
    blk8 = tt // 8
    out_w = lambda w: jax.ShapeDtypeStruct((b, s, w), F32)
    const = lambda a: pl.BlockSpec(a.shape, lambda bi, i: (0,) * a.ndim)
    ones = _head_ones(W_GDN)
    eb, tri = jnp.asarray(eb), jnp.asarray(tri)
    ospec = lambda w: pl.BlockSpec((None, tt, w), lambda bi, i: (bi, i, 0))
    return pl.pallas_call(
        functools.partial(_gdn_prep_kernel, nt),
        out_shape=(out_w(W_GDN), out_w(W_GDN), out_w(W_GDN), out_w(W_CHAIN), out_w(W_CHAIN)),
        grid=(b, nt),
        in_specs=[pl.BlockSpec((None, tt, w3), lambda bi, i: (bi, i, 0)),
                  pl.BlockSpec((None, 8, w3), lambda bi, i: (bi, jnp.maximum(i * blk8 - 1, 0), 0)),
                  pl.BlockSpec((None, 8, w3), lambda bi, i: (bi, jnp.minimum((i + 1) * blk8, s // 8 - 1), 0)),
                  pl.BlockSpec((None, tt, LANE), lambda bi, i: (bi, i, C_BA // LANE)),
                  const(cw), const(alog_row), const(dtb_row), const(ones), const(eb), const(tri)],
        out_specs=(ospec(W_GDN), ospec(W_GDN), ospec(W_GDN), ospec(W_CHAIN), ospec(W_CHAIN)),
        compiler_params=_cparams(("parallel", "parallel")),
        name="gdn_prep",
    )(proj3, proj3, proj3, proj3, cw, alog_row, dtb_row, ones, eb, tri)


GDN_CS = 4


def _gdn_consts():
    c = GDN_CHUNK
    i = np.arange(c)[:, None]
    lane = np.arange(W_CHAIN)[None, :]
    j = lane % HEAD_DIM
    fwd = lane < W_GDN
    incl = np.where(fwd, j <= i, j >= i)
    strict = np.where(fwd, j < i, j > i)
    eye = (j == i)
    masks = np.stack([incl, strict, eye]).astype(np.float32)
    r = np.arange(CHAIN_TILE)[:, None] // HEAD_DIM
    cidx = np.arange(CHAIN_TILE)[None, :] // HEAD_DIM
    bd = (r == cidx).astype(np.float32)
    return jnp.asarray(masks), jnp.asarray(bd)


def _gdn_kernel(qf_ref, kf_ref, vf_ref, qb_ref, kb_ref, vb_ref, betaf_ref, betab_ref, gcf_ref, gcb_ref,
                masks_ref, bd_ref, of_ref, ob_ref, state_ref):
    @pl.when(pl.program_id(1) == 0)
    def _():
        state_ref[...] = jnp.zeros_like(state_ref)

    c = GDN_CHUNK
    ntile = W_CHAIN // CHAIN_TILE
    incl = masks_ref[0]
    strict = masks_ref[1]
    eye = masks_ref[2]
    bd = bd_ref[...]
    reps = CHAIN_TILE // HEAD_DIM

    def bdiag(r_t):
        return jnp.concatenate([r_t] * reps, axis=0) * bd

    def chain_mm(lhs, rhs):
        outs = []
        for t in range(ntile):
            sl = slice(t * CHAIN_TILE, (t + 1) * CHAIN_TILE)
            outs.append(_dot(lhs[:, sl], bdiag(rhs[:, sl])))
        return jnp.concatenate(outs, axis=1)

    lane_chain = lax.broadcasted_iota(I32, (1, CHAIN_TILE), 1) // HEAD_DIM
    cmasks = [(lane_chain == h).astype(F32) for h in range(reps)]

    for step in range(GDN_CS):
        fs = slice(step * c, (step + 1) * c)
        bs = slice((GDN_CS - 1 - step) * c, (GDN_CS - step) * c)
        cat = lambda a, bb: jnp.concatenate([a, bb], axis=1)
        q = cat(qf_ref[fs, :], qb_ref[bs, :])
        k = cat(kf_ref[fs, :], kb_ref[bs, :])
        v = cat(vf_ref[fs, :], vb_ref[bs, :])
        beta = cat(betaf_ref[fs, :W_GDN], betab_ref[bs, W_GDN:])
        gcol = cat(gcf_ref[fs, :W_GDN], gcb_ref[bs, W_GDN:])
        grow = jnp.sum(gcol * eye, axis=0, keepdims=True)
        glast = cat(gcol[c - 1:c, :W_GDN], gcol[0:1, W_GDN:])
        decay = jnp.where(incl > 0, jnp.exp(jnp.where(incl > 0, gcol - grow, 0.0)), 0.0)
        kb = k * beta
        lhs = jnp.concatenate([kb, q], axis=0)
        sc = []
        for t in range(ntile):
            sl = slice(t * CHAIN_TILE, (t + 1) * CHAIN_TILE)
            kst = jnp.concatenate([k[:, sl] * m for m in cmasks], axis=0)
            sc.append(_dot(lhs[:, sl], kst, NT))
        sc = jnp.concatenate(sc, axis=1)
        a_mat = sc[:c] * decay * strict
        intra = sc[c:] * decay
        p = eye - a_mat
        x = chain_mm(a_mat, a_mat)
        for it in range(5):
            if it < 4:
                both = chain_mm(jnp.concatenate([p, x], axis=0), x)
                p = p + both[:c]
                x = both[c:]
            else:
                p = p + chain_mm(p, x)
        egc = jnp.exp(gcol)
        w = chain_mm(p, kb * egc)
        u = chain_mm(p, v * beta)
        q_dec = q * egc
        k_dec = k * jnp.exp(glast - gcol)
        g_last = jnp.exp(glast)
        lhs2 = jnp.concatenate([w, q_dec], axis=0)
        ws_qs = jnp.concatenate(
            [_dot(lhs2[:, t * CHAIN_TILE:(t + 1) * CHAIN_TILE], state_ref[t]) for t in range(ntile)], axis=1)
        v_new = u - ws_qs[:c]
        o = ws_qs[c:] + chain_mm(intra, v_new)
        for t in range(ntile):
            sl = slice(t * CHAIN_TILE, (t + 1) * CHAIN_TILE)
            state_ref[t] = state_ref[t] * g_last[:, sl] + _dot(k_dec[:, sl], v_new[:, sl], TN) * bd
        of_ref[fs, :] = o[:, :W_GDN]
        ob_ref[bs, :] = o[:, W_GDN:]


def _gdn_main(qn, kn, vv, beta, gc):
    b, s, _ = qn.shape
    ts = GDN_CS * GDN_CHUNK
    nb = s // ts
    masks, bd = _gdn_consts()
    fwd = lambda w: pl.BlockSpec((None, ts, w), lambda bi, i: (bi, i, 0))
    bwd = lambda w: pl.BlockSpec((None, ts, w), lambda bi, i: (bi, nb - 1 - i, 0))
    const = lambda a: pl.BlockSpec(a.shape, lambda bi, i: (0,) * a.ndim)
    out = jax.ShapeDtypeStruct((b, s, W_GDN), F32)
    return pl.pallas_call(
        _gdn_kernel,
        out_shape=(out, out),
        grid=(b, nb),
        in_specs=[fwd(W_GDN), fwd(W_GDN), fwd(W_GDN), bwd(W_GDN), bwd(W_GDN), bwd(W_GDN),
                  fwd(W_CHAIN), bwd(W_CHAIN), fwd(W_CHAIN), bwd(W_CHAIN), const(masks), const(bd)],
        out_specs=(fwd(W_GDN), bwd(W_GDN)),
        scratch_shapes=[pltpu.VMEM((W_CHAIN // CHAIN_TILE, CHAIN_TILE, CHAIN_TILE), F32)],
        compiler_params=_cparams(("parallel", "arbitrary")),
        name="gdn_scan",
    )(qn, kn, vv, qn, kn, vv, beta, beta, gc, gc, masks, bd)


def _layer_norm(y, g, b):
    mu = jnp.mean(y, axis=-1, keepdims=True)
    d = y - mu
    var = jnp.mean(d * d, axis=-1, keepdims=True)
    return d * lax.rsqrt(var + LN_EPS) * g + b


def _merge_kernel(x_ref, gates_ref, na_ref, n0, m0, l0, n1, m1, l1, n2, m2, l2, of_ref, ob_ref, z_ref,
                  wna_ref, wdil_ref, wgdn_ref, wout_ref, normw_ref, ones_ref, g_ref, b_ref, wr_ref,
                  x1_ref, xa_ref, aff_ref):
    mm = jnp.maximum(jnp.maximum(m0[...], m1[...]), m2[...])
    a0, a1, a2 = jnp.exp(m0[...] - mm), jnp.exp(m1[...] - mm), jnp.exp(m2[...] - mm)
    o_dil = (a0 * n0[...] + a1 * n1[...] + a2 * n2[...]) / (a0 * l0[...] + a1 * l1[...] + a2 * l2[...])
    o = of_ref[...] + ob_ref[...]
    ms = _dot(o * o, ones_ref[...], hi=True) * (1.0 / HEAD_DIM)
    z = z_ref[...]
    o_gdn = o * lax.rsqrt(ms + NORM_EPS) * normw_ref[...] * (z * _sigmoid(z))
    br_na = _dot(na_ref[...], wna_ref[...])
    br_dil = _dot(o_dil, wdil_ref[...])
    br_gdn = _dot(o_gdn, wgdn_ref[...])
    gate = _sigmoid(gates_ref[...])
    merged = (gate[:, :D_MODEL] * br_na + gate[:, D_MODEL:2 * D_MODEL] * br_dil
              + gate[:, 2 * D_MODEL:] * br_gdn)
    mix = _dot(merged, wout_ref[...])
    x1 = _layer_norm(DN_ALPHA * x_ref[...] + mix, g_ref[...], b_ref[...])
    x1_ref[...] = x1
    xa_ref[...] = DN_ALPHA * x1
    logits = _dot(x1, wr_ref[...], hi=True)
    e = jnp.exp(logits - jnp.max(logits, axis=-1, keepdims=True))
    aff_ref[...] = e / jnp.sum(e, axis=-1, keepdims=True)


def _merge(x, proj, o_na, dil, o_f, o_b, w_br_na, w_br_dil, w_br_gdn, w_out, norm_w, ln_g, ln_b, w_router, tm=256):
    n = x.shape[0]
    tm = min(tm, n)
    tok = lambda w, c=0: pl.BlockSpec((tm, w), lambda i: (i, c))
    const = lambda a: pl.BlockSpec(a.shape, lambda i: (0,) * a.ndim)
    normw = jnp.tile(norm_w, H_GDN).reshape(1, W_GDN)
    ones = _head_ones(W_GDN)
    consts = [w_br_na.astype(BF16), w_br_dil.astype(BF16), w_br_gdn.astype(BF16), w_out.astype(BF16),
              normw, ones, ln_g.reshape(1, -1), ln_b.reshape(1, -1), w_router]
    dil_flat = [t for grp in dil for t in grp]
    big = jax.ShapeDtypeStruct((n, D_MODEL), F32)
    return pl.pallas_call(
        _merge_kernel,
        out_shape=(big, big, jax.ShapeDtypeStruct((n, N_EXPERTS), F32)),
        grid=(n // tm,),
        in_specs=[tok(D_MODEL), tok(N_BRANCH * D_MODEL, C_GATE // (N_BRANCH * D_MODEL)), tok(W_NA)]
                 + [tok(LANE)] * 9 + [tok(W_GDN), tok(W_GDN), tok(W_GDN, C_Z // W_GDN)]
                 + [const(a) for a in consts],
        out_specs=(tok(D_MODEL), tok(D_MODEL), tok(N_EXPERTS)),
        compiler_params=_cparams(("parallel",)),
        name="merge_ln1_router",
    )(x, proj, o_na, *dil_flat, o_f, o_b, proj, *consts)


def _thresh_kernel(cap, aff_ref, tau_ref, cgt_ref):
    bits = pltpu.bitcast(aff_ref[...], I32)
    ne = bits.shape[0]

    def count_ge(t):
        return jnp.sum((bits >= t).astype(I32), axis=1, keepdims=True)

    def body(_, carry):
        lo, hi = carry
        mid = lo + jnp.right_shift(hi - lo, 1)
        ok = count_ge(mid) >= cap
        return jnp.where(ok, mid, lo), jnp.where(ok, hi, mid)

    lo0 = jnp.zeros((ne, 1), I32)
    hi0 = jnp.full((ne, 1), 0x7F800000, I32)
    lo, _ = lax.fori_loop(0, 31, body, (lo0, hi0))
    tau_ref[...] = jnp.broadcast_to(lo, tau_ref.shape)
    cgt_ref[...] = jnp.broadcast_to(jnp.sum((bits > lo).astype(I32), axis=1, keepdims=True), cgt_ref.shape)


def _thresholds(aff_t, cap):
    ne, n = aff_t.shape
    out = jax.ShapeDtypeStruct((ne, LANE), I32)
    return pl.pallas_call(
        functools.partial(_thresh_kernel, cap),
        out_shape=(out, out),
        in_specs=[pl.BlockSpec((ne, n), lambda: (0, 0))],
        out_specs=(pl.BlockSpec((ne, LANE), lambda: (0, 0)),) * 2,
        compiler_params=pltpu.CompilerParams(vmem_limit_bytes=VMEM_LIMIT),
        name="expert_thresholds",
    )(aff_t)


CMP_TC = 256
CMP_WIN = CMP_TC // LANE + 1


def _compact_kernel(aff_ref, tau_ref, take_ref, tri_ref, idx_ref, gate_ref, carry_ref):
    step = pl.program_id(0)

    @pl.when(step == 0)
    def _():
        idx_ref[...] = jnp.zeros_like(idx_ref)
        gate_ref[...] = jnp.zeros_like(gate_ref)
        carry_ref[...] = jnp.zeros_like(carry_ref)

    aff = aff_ref[...]
    tc = aff.shape[0]
    bits = pltpu.bitcast(aff, I32)
    tau = tau_ref[...]
    gt = (bits > tau).astype(F32)
    eq = (bits == tau).astype(F32)
    tri = tri_ref[...]
    carry = carry_ref[...]
    tie_incl = _dot(tri, eq) + carry[1:2, :]
    take = eq * ((tie_incl - eq) < take_ref[...].astype(F32)).astype(F32)
    sel = gt + take
    pos_incl = _dot(tri, sel) + carry[0:1, :]
    pos = pos_incl - sel
    carry_ref[0:1, :] = pos_incl[tc - 1:tc, :]
    carry_ref[1:2, :] = tie_incl[tc - 1:tc, :]
    tok = step * tc + lax.broadcasted_iota(I32, (tc, 1), 0)
    lane = lax.broadcasted_iota(I32, (1, LANE), 1)
    for e in range(N_EXPERTS):
        start = carry[0, e].astype(I32)
        r0 = start // LANE
        rel = pos[:, e:e + 1].astype(I32) - r0 * LANE
        sel_e = sel[:, e:e + 1] > 0
        aff_e = aff[:, e:e + 1]
        for j in range(CMP_WIN):
            hit = (rel == lane + j * LANE) & sel_e
            idx_ref[e, r0 + j] += jnp.sum(jnp.where(hit, tok, 0), axis=0, keepdims=True)
            gate_ref[e, r0 + j] += jnp.sum(jnp.where(hit, aff_e, 0.0), axis=0, keepdims=True)


def _compact(aff, tau_row, take_row, cap):
    n, ne = aff.shape
    tc = min(CMP_TC, n)
    rows = cap // LANE + CMP_WIN
    t = np.arange(tc)
    tri = jnp.asarray((t[None, :] <= t[:, None]).astype(np.float32))
    idx, gate = pl.pallas_call(
        _compact_kernel,
        out_shape=(jax.ShapeDtypeStruct((ne, rows, 1, LANE), I32),
                   jax.ShapeDtypeStruct((ne, rows, 1, LANE), F32)),
        grid=(n // tc,),
        in_specs=[pl.BlockSpec((tc, ne), lambda i: (i, 0)),
                  pl.BlockSpec((1, ne), lambda i: (0, 0)),
                  pl.BlockSpec((1, ne), lambda i: (0, 0)),
                  pl.BlockSpec((tc, tc), lambda i: (0, 0))],
        out_specs=(pl.BlockSpec((ne, rows, 1, LANE), lambda i: (0, 0, 0, 0)),) * 2,
        scratch_shapes=[pltpu.VMEM((2, ne), F32)],
        compiler_params=_cparams(("arbitrary",)),
        name="expert_compaction",
    )(aff, tau_row, take_row, tri)
    idx = idx[:, :cap // LANE].reshape(ne, cap)
    gate = gate[:, :cap // LANE].reshape(ne, cap)
    return idx, gate


MOE_TM = 256


def _moe_kernel(idx_hbm, gate_ref, x_hbm, acc_in, wg_ref, wu_ref, wd_ref, acc_hbm, idx_smem, xbuf, ybuf, sems):
    del acc_in
    e = pl.program_id(0)
    k = pl.program_id(1)
    tm = xbuf.shape[0]
    pltpu.sync_copy(idx_hbm.at[e, k], idx_smem)

    def gather(r, _):
        t = idx_smem[r]
        pltpu.make_async_copy(x_hbm.at[pl.ds(t, 1), :], xbuf.at[pl.ds(r, 1), :], sems.at[0]).start()
        pltpu.make_async_copy(acc_hbm.at[pl.ds(t, 1), :], ybuf.at[pl.ds(r, 1), :], sems.at[1]).start()
        return 0

    lax.fori_loop(0, tm, gather, 0)
    pltpu.make_async_copy(x_hbm.at[pl.ds(0, tm), :], xbuf, sems.at[0]).wait()
    pltpu.make_async_copy(acc_hbm.at[pl.ds(0, tm), :], ybuf, sems.at[1]).wait()

    xb = xbuf[...].astype(BF16)
    hg = jnp.dot(xb, wg_ref[...], preferred_element_type=F32)
    hu = jnp.dot(xb, wu_ref[...], preferred_element_type=F32)
    hid = (hg * _sigmoid(hg) * hu).astype(BF16)
    ye = jnp.dot(hid, wd_ref[...], preferred_element_type=F32)
    ybuf[...] = ybuf[...] + ye * gate_ref[...]

    def scatter(r, _):
        t = idx_smem[r]
        pltpu.make_async_copy(ybuf.at[pl.ds(r, 1), :], acc_hbm.at[pl.ds(t, 1), :], sems.at[2]).start()
        return 0

    lax.fori_loop(0, tm, scatter, 0)
    pltpu.make_async_copy(ybuf, acc_hbm.at[pl.ds(0, tm), :], sems.at[2]).wait()


def _moe(x1, acc, idx, gate, wg, wu, wd):
    n, d = x1.shape
    ne, cap = idx.shape
    tm = min(MOE_TM, cap)
    nk = cap // tm
    f = wg.shape[2]
    idx3 = idx.reshape(ne, nk, tm)
    gate4 = gate.reshape(ne, nk, tm, 1)
    return pl.pallas_call(
        _moe_kernel,
        out_shape=jax.ShapeDtypeStruct((n, d), F32),
        grid=(ne, nk),
        in_specs=[pl.BlockSpec(memory_space=pl.ANY),
                  pl.BlockSpec((None, None, tm, 1), lambda e, k: (e, k, 0, 0)),
                  pl.BlockSpec(memory_space=pl.ANY),
                  pl.BlockSpec(memory_space=pl.ANY),
                  pl.BlockSpec((None, d, f), lambda e, k: (e, 0, 0)),
                  pl.BlockSpec((None, d, f), lambda e, k: (e, 0, 0)),
                  pl.BlockSpec((None, f, d), lambda e, k: (e, 0, 0))],
        out_specs=pl.BlockSpec(memory_space=pl.ANY),
        scratch_shapes=[pltpu.SMEM((tm,), I32), pltpu.VMEM((tm, d), F32), pltpu.VMEM((tm, d), F32),
                        pltpu.SemaphoreType.DMA((3,))],
        input_output_aliases={3: 0},
        compiler_params=pltpu.CompilerParams(dimension_semantics=("arbitrary", "arbitrary"),
                                             vmem_limit_bytes=VMEM_LIMIT, has_side_effects=True),
        name="expert_ffn",
    )(idx3, gate4, x1, acc, wg, wu, wd)


def _ln_kernel(x_ref, g_ref, b_ref, o_ref):
    o_ref[...] = _layer_norm(x_ref[...], g_ref[...], b_ref[...])


def _ln(x, g, b, tm=512):
    n, d = x.shape
    tm = min(tm, n)
    return pl.pallas_call(
        _ln_kernel,
        out_shape=jax.ShapeDtypeStruct((n, d), F32),
        grid=(n // tm,),
        in_specs=[pl.BlockSpec((tm, d), lambda i: (i, 0)), pl.BlockSpec((1, d), lambda i: (0, 0)),
                  pl.BlockSpec((1, d), lambda i: (0, 0))],
        out_specs=pl.BlockSpec((tm, d), lambda i: (i, 0)),
        compiler_params=_cparams(("parallel",)),
        name="layer_norm2",
    )(x, g.reshape(1, d), b.reshape(1, d))


def _token_mixer_parts(x, b, s, w_in_p, bias_tab, conv_w, a_log, dt_bias, rope_tabs):
    n = b * s
    proj = _inproj(x, w_in_p)
    proj3 = proj.reshape(b, s, D_PAD)
    o_na = _na_attention(proj3, bias_tab).reshape(n, W_NA)
    qr, kr, vb = _rope(proj, s, *rope_tabs)
    dil = [_dil_group(qr, kr, vb, b, s, g) for g in range(len(DIL_PATTERN))]
    qn, kn, vv, beta, gc = _gdn_prep(proj3, conv_w, a_log, dt_bias)
    o_f, o_b = _gdn_main(qn, kn, vv, beta, gc)
    return proj, o_na, dil, o_f.reshape(n, W_GDN), o_b.reshape(n, W_GDN)


def _expert_choice(x1, xa, aff, wg, wu, wd):
    n = x1.shape[0]
    cap = EC_FACTOR * n // N_EXPERTS
    tau, cgt = _thresholds(aff.T, cap)
    tau_row = tau[:, 0].reshape(1, N_EXPERTS)
    take_row = (cap - cgt[:, 0]).reshape(1, N_EXPERTS)
    idx, gate = _compact(aff, tau_row, take_row, cap)
    return _moe(x1, xa, idx, gate, wg, wu, wd)


def _trunk(x3, params, shared):
    b, s, d = x3.shape
    n = b * s
    x = x3.reshape(n, d)
    rope_tabs = _rope_tables(s)
    for l in range(DEPTH):
        p = {k: v[l] for k, v in params.items()}
        sh = shared[l]
        proj, o_na, dil, o_f, o_b = _token_mixer_parts(x, b, s, sh["w_in"], sh["bias_tab"], p["conv_w"],
                                                       p["a_log"], p["dt_bias"], rope_tabs)
        x1, xa, aff = _merge(x, proj, o_na, dil, o_f, o_b, p["w_br_na"], p["w_br_dil"], p["w_br_gdn"],
                             p["w_out"], p["gdn_norm_w"], p["ln1_g"], p["ln1_b"], p["w_router"])
        acc = _expert_choice(x1, xa, aff, sh["w_gate"], sh["w_up"], sh["w_down"])
        x = _ln(acc, p["ln2_g"], p["ln2_b"])
    return x.reshape(b, s, d)


def kernel(x_prompt, x_sample, w_in, na_rpb, conv_w, a_log, dt_bias, gdn_norm_w, w_br_na, w_br_dil, w_br_gdn,
           w_out, ln1_g, ln1_b, w_router, w_up, w_gate, w_down, ln2_g, ln2_b):
    params = dict(conv_w=conv_w, a_log=a_log, dt_bias=dt_bias, gdn_norm_w=gdn_norm_w, w_br_na=w_br_na,
                  w_br_dil=w_br_dil, w_br_gdn=w_br_gdn, w_out=w_out, ln1_g=ln1_g, ln1_b=ln1_b,
                  w_router=w_router, ln2_g=ln2_g, ln2_b=ln2_b)
    shared = [dict(w_in=_reorder_w_in(w_in[l]), bias_tab=_na_bias_table(na_rpb[l]),
                   w_gate=w_gate[l].astype(BF16), w_up=w_up[l].astype(BF16), w_down=w_down[l].astype(BF16))
              for l in range(DEPTH)]
    return _trunk(x_prompt, params, shared), _trunk(x_sample, params, shared)
```

```python
import functools
import math

import numpy as np
import jax
import jax.numpy as jnp
from jax import lax
from jax.experimental import pallas as pl
from jax.experimental.pallas import tpu as pltpu

F32 = jnp.float32
BF16 = jnp.bfloat16
I32 = jnp.int32

D_MODEL = 1024
DEPTH = 2
HEAD_DIM = 64
H_NA = 4
H_DIL = 6
H_GDN = 6
W_NA = H_NA * HEAD_DIM
W_DIL = H_DIL * HEAD_DIM
W_GDN = H_GDN * HEAD_DIM
GRID_W = 64
NA_KH = 8
NA_KW = 16
DIL_PATTERN = ((128, 1), (512, 4), (2048, 16))
DIL_HPG = H_DIL // len(DIL_PATTERN)
W_DIL_OUT = DIL_HPG * HEAD_DIM
ROPE_THETA = 10000.0
GDN_CHUNK = 64
CONV_K = 5
N_EXPERTS = 16
D_EXPERT = 2048
EC_FACTOR = 2
N_BRANCH = 3
DN_ALPHA = (2 * DEPTH) ** 0.25
LN_EPS = 1e-5
NORM_EPS = 1e-6
NEG_BIG = -1e30

C_GDN = 0
C_DIL = C_GDN + 3 * W_GDN
C_NA = C_DIL + 3 * W_DIL
C_GATE = C_NA + 3 * W_NA
C_Z = C_GATE + N_BRANCH * D_MODEL
C_BA = C_Z + W_GDN
LANE = 128
D_PAD = C_BA + LANE
VMEM_LIMIT = 56 * 1024 * 1024


def _cparams(sem):
    return pltpu.CompilerParams(dimension_semantics=sem, vmem_limit_bytes=VMEM_LIMIT)


def _dot(a, b, dims=None, hi=False):
    if dims is None:
        dims = (((a.ndim - 1,), (0,)), ((), ()))
    if hi:
        return lax.dot_general(a.astype(F32), b.astype(F32), dims, precision=lax.Precision.HIGHEST,
                               preferred_element_type=F32)
    return lax.dot_general(a.astype(BF16), b.astype(BF16), dims, preferred_element_type=F32)


NT = (((1,), (1,)), ((), ()))
TN = (((0,), (0,)), ((), ()))


def _sigmoid(x):
    return 1.0 / (1.0 + jnp.exp(-x))


def _inproj_kernel(x_ref, w_ref, o_ref, xb_ref):
    @pl.when(pl.program_id(1) == 0)
    def _():
        xb_ref[...] = x_ref[...].astype(BF16)

    o_ref[...] = jnp.dot(xb_ref[...], w_ref[...], preferred_element_type=F32)


def _inproj(x, w_bf16, tm=1024, tn=512):
    n, k = x.shape
    npad = w_bf16.shape[1]
    tm = min(tm, n)
    return pl.pallas_call(
        _inproj_kernel,
        out_shape=jax.ShapeDtypeStruct((n, npad), F32),
        grid=(n // tm, npad // tn),
        in_specs=[pl.BlockSpec((tm, k), lambda i, j: (i, 0)),
                  pl.BlockSpec((k, tn), lambda i, j: (0, j))],
        out_specs=pl.BlockSpec((tm, tn), lambda i, j: (i, j)),
        scratch_shapes=[pltpu.VMEM((tm, k), BF16)],
        compiler_params=_cparams(("parallel", "arbitrary")),
        name="inproj",
    )(x, w_bf16)


def _reorder_w_in(w):
    o_qd = 3 * W_NA
    o_qc = o_qd + 3 * W_DIL
    o_z = o_qc + 3 * W_GDN
    o_b = o_z + W_GDN
    o_gate = o_b + 4 * H_GDN
    parts = [w[:, o_qc:o_z], w[:, o_qd:o_qc], w[:, 0:o_qd], w[:, o_gate:], w[:, o_z:o_b], w[:, o_b:o_gate],
             jnp.zeros((w.shape[0], LANE - 4 * H_GDN), w.dtype)]
    return jnp.concatenate(parts, axis=1).astype(BF16)


NA_RQ = 8
NA_WIN_ROWS = 16
NA_KB = 256


def _toeplitz_kernel(rpb_ref, onehot_ref, o_ref):
    o_ref[...] = _dot(rpb_ref[...], onehot_ref[...], hi=True)


def _na_bias_table(rpb):
    h, ndr, ndc = rpb.shape
    qc = np.arange(GRID_W)[:, None]
    kc = np.arange(GRID_W)[None, :]
    dc = np.clip(kc - qc, -(NA_KW - 1), NA_KW - 1) + (NA_KW - 1)
    onehot = (np.arange(LANE)[:, None, None] == dc[None]).astype(np.float32)
    onehot = jnp.asarray(onehot.reshape(LANE, GRID_W * GRID_W))
    rows = -(-h * ndr // 8) * 8
    rpb2 = jnp.zeros((rows, LANE), F32).at[:h * ndr, :ndc].set(rpb.reshape(h * ndr, ndc))
    toe = pl.pallas_call(
        _toeplitz_kernel,
        out_shape=jax.ShapeDtypeStruct((rows, GRID_W * GRID_W), F32),
        name="na_bias_toeplitz",
    )(rpb2, onehot)
    toe = toe[:h * ndr].reshape(h, ndr, GRID_W, GRID_W)
    ws = np.clip(qc - NA_KW // 2, 0, GRID_W - NA_KW)
    ok = jnp.asarray((kc >= ws) & (kc < ws + NA_KW))[None, :, None, :]
    tabs = []
    for off in range(NA_KH):
        lo = NA_KH - 1 - off
        t = jnp.transpose(toe[:, lo:lo + NA_KH], (0, 2, 1, 3))
        tabs.append(jnp.where(ok, t, NEG_BIG).reshape(h * GRID_W, NA_KH * GRID_W))
    return jnp.stack(tabs)


def _na_kernel(rows, q_ref, k0, k1, k2, k3, v0, v1, v2, v3, bias_ref, o_ref, kc_ref, vc_ref):
    rq = pl.program_id(1)
    for j, (kr, vr) in enumerate(((k0, v0), (k1, v1), (k2, v2), (k3, v3))):
        kc_ref[j * NA_KB:(j + 1) * NA_KB, :] = kr[...].astype(BF16)
        vc_ref[j * NA_KB:(j + 1) * NA_KB, :] = vr[...].astype(BF16)
    ws = jnp.clip(rq * NA_RQ - NA_KH // 2, 0, rows - NA_WIN_ROWS)
    lane = lax.broadcasted_iota(I32, (1, W_NA), 1)
    masks = [(lane // HEAD_DIM == h).astype(F32) for h in range(H_NA)]
    for j in range(NA_RQ):
        r = rq * NA_RQ + j
        r0 = jnp.clip(r - NA_KH // 2, 0, rows - NA_KH)
        koff = pl.multiple_of((r0 - ws) * GRID_W, GRID_W)
        off = r - r0
        qj = q_ref[j * GRID_W:(j + 1) * GRID_W, :] * (HEAD_DIM ** -0.5)
        qs = jnp.concatenate([qj * m for m in masks], axis=0).astype(BF16)
        kw = kc_ref[pl.ds(koff, NA_KH * GRID_W), :]
        vw = vc_ref[pl.ds(koff, NA_KH * GRID_W), :]
        s = _dot(qs, kw, NT) + bias_ref[off]
        m = jnp.max(s, axis=-1, keepdims=True)
        p = jnp.exp(s - m)
        den = jnp.sum(p, axis=-1, keepdims=True)
        o = _dot(p, vw) / den
        acc = o[0:GRID_W] * masks[0]
        for h in range(1, H_NA):
            acc = acc + o[h * GRID_W:(h + 1) * GRID_W] * masks[h]
        o_ref[j * GRID_W:(j + 1) * GRID_W, :] = acc


def _na_attention(proj3, bias_tab):
    b, s, _ = proj3.shape
    rows = s // GRID_W
    nrq = rows // NA_RQ
    tq = NA_RQ * GRID_W
    cq, ck, cv = C_NA // W_NA, C_NA // W_NA + 1, C_NA // W_NA + 2
    nkb = s // NA_KB

    def kmap(j, col):
        def f(bi, rq):
            kb0 = jnp.clip(2 * rq - 1, 0, nkb - NA_WIN_ROWS * GRID_W // NA_KB)
            return (bi, kb0 + j, col)
        return f

    in_specs = [pl.BlockSpec((None, tq, W_NA), lambda bi, rq: (bi, rq, cq))]
    in_specs += [pl.BlockSpec((None, NA_KB, W_NA), kmap(j, ck)) for j in range(4)]
    in_specs += [pl.BlockSpec((None, NA_KB, W_NA), kmap(j, cv)) for j in range(4)]
    in_specs += [pl.BlockSpec(bias_tab.shape, lambda bi, rq: (0, 0, 0))]
    return pl.pallas_call(
        functools.partial(_na_kernel, rows),
        out_shape=jax.ShapeDtypeStruct((b, s, W_NA), F32),
        grid=(b, nrq),
        in_specs=in_specs,
        out_specs=pl.BlockSpec((None, tq, W_NA), lambda bi, rq: (bi, rq, 0)),
        scratch_shapes=[pltpu.VMEM((NA_WIN_ROWS * GRID_W, W_NA), BF16),
                        pltpu.VMEM((NA_WIN_ROWS * GRID_W, W_NA), BF16)],
        compiler_params=_cparams(("parallel", "parallel")),
        name="na_attention",
    )(proj3, *([proj3] * 8), bias_tab)


def _rope_tables(s):
    half = HEAD_DIM // 2
    inv = ROPE_THETA ** (-jnp.arange(half, dtype=F32) / half)
    ang = jnp.arange(s, dtype=F32)[:, None] * inv[None, :]
    cos, sin = jnp.cos(ang), jnp.sin(ang)
    cos_t = jnp.concatenate([cos, cos, cos, cos], axis=1)
    sin_t = jnp.concatenate([-sin, sin, -sin, sin], axis=1)
    return cos_t, sin_t


def _rope_kernel(q_ref, k_ref, v_ref, cos_ref, sin_ref, *refs):
    ngrp = len(DIL_PATTERN)
    outs, stage_ref = refs[:3 * ngrp], refs[3 * ngrp]
    tt = q_ref.shape[0]
    cos = jnp.concatenate([cos_ref[...]] * ngrp, axis=1)
    sin = jnp.concatenate([sin_ref[...]] * ngrp, axis=1)
    lane = lax.broadcasted_iota(I32, (1, W_DIL), 1)
    first = (lane % HEAD_DIM) < (HEAD_DIM // 2)

    def rot(x):
        up = pltpu.roll(x, W_DIL - HEAD_DIM // 2, 1)
        dn = pltpu.roll(x, HEAD_DIM // 2, 1)
        return x * cos + jnp.where(first, up, dn) * sin

    qkv = (rot(q_ref[...].astype(F32)) * (HEAD_DIM ** -0.5), rot(k_ref[...].astype(F32)), v_ref[...].astype(F32))
    for g, (_, dil) in enumerate(DIL_PATTERN):
        for j, val in enumerate(qkv):
            if dil == 1:
                outs[3 * g + j][0] = val[:, g * LANE:(g + 1) * LANE].astype(BF16)
                continue
            stage_ref[j] = val[:, g * LANE:(g + 1) * LANE]
            for c in range(dil):
                outs[3 * g + j][c] = stage_ref[j, pl.ds(c, tt // dil, stride=dil), :].astype(BF16)


def _rope(proj3, cos_t, sin_t, tt=512):
    b, s, _ = proj3.shape
    tt = min(tt, s)
    c0 = C_DIL // W_DIL
    spec = lambda c: pl.BlockSpec((None, tt, W_DIL), lambda bi, i: (bi, i, c))
    tspec = pl.BlockSpec((tt, LANE), lambda bi, i: (i, 0))
    out_shape, out_specs = [], []
    for _, dil in DIL_PATTERN:
        for _ in range(3):
            out_shape.append(jax.ShapeDtypeStruct((b, dil, s // dil, LANE), BF16))
            out_specs.append(pl.BlockSpec((None, dil, tt // dil, LANE), lambda bi, i: (bi, 0, i, 0)))
    outs = pl.pallas_call(
        _rope_kernel,
        out_shape=tuple(out_shape),
        grid=(b, s // tt),
        in_specs=[spec(c0), spec(c0 + 1), spec(c0 + 2), tspec, tspec],
        out_specs=tuple(out_specs),
        scratch_shapes=[pltpu.VMEM((3, tt, LANE), F32)],
        compiler_params=_cparams(("parallel", "parallel")),
        name="rope",
    )(proj3, proj3, proj3, cos_t, sin_t)
    return [outs[3 * g:3 * g + 3] for g in range(len(DIL_PATTERN))]


DIL_QB = 256


def _dil_kernel(length, half, q_ref, kp_ref, kc_ref, kn_ref, vp_ref, vc_ref, vn_ref, o_ref, lse_ref):
    i = pl.program_id(2)
    qb = q_ref.shape[0]
    hb = kp_ref.shape[0]
    lane = lax.broadcasted_iota(I32, (1, LANE), 1)
    m0 = (lane < HEAD_DIM)
    q = q_ref[...]
    zero = jnp.zeros_like(q)
    qs = jnp.concatenate([jnp.where(m0, q, zero), jnp.where(m0, zero, q)], axis=0)
    kcat = jnp.concatenate([kp_ref[...], kc_ref[...], kn_ref[...]], axis=0)
    vcat = jnp.concatenate([vp_ref[...], vc_ref[...], vn_ref[...]], axis=0)
    s = _dot(qs, kcat, NT)
    qpos = i * qb + lax.broadcasted_iota(I32, (2 * qb, 1), 0) % qb
    kpos = i * qb - hb + lax.broadcasted_iota(I32, (1, qb + 2 * hb), 1)
    valid = (jnp.abs(qpos - kpos) <= half) & (kpos >= 0) & (kpos < length)
    s = jnp.where(valid, s, NEG_BIG)
    m = jnp.max(s, axis=-1, keepdims=True)
    p = jnp.where(valid, jnp.exp(s - m), 0.0)
    den = jnp.sum(p, axis=-1, keepdims=True)
    o = _dot(p, vcat) / den
    lse = m + jnp.log(den)
    o_ref[...] = jnp.where(m0, o[:qb], o[qb:])
    lse_ref[...] = jnp.where(m0, lse[:qb], lse[qb:])


def _dil_group(q, k, v, g):
    win, dil = DIL_PATTERN[g]
    half = win // (2 * dil)
    b, _, length, _ = q.shape
    qb = min(DIL_QB, length)
    hb = min(LANE, qb)
    assert half <= hb
    ratio = qb // hb
    nhb = length // hb
    cur = pl.BlockSpec((None, None, qb, LANE), lambda bi, c, i: (bi, c, i, 0))
    prev = pl.BlockSpec((None, None, hb, LANE), lambda bi, c, i: (bi, c, jnp.maximum(i * ratio - 1, 0), 0))
    nxt = pl.BlockSpec((None, None, hb, LANE), lambda bi, c, i: (bi, c, jnp.minimum((i + 1) * ratio, nhb - 1), 0))
    out = jax.ShapeDtypeStruct((b, dil, length, LANE), F32)
    return pl.pallas_call(
        functools.partial(_dil_kernel, length, half),
        out_shape=(out, out),
        grid=(b, dil, length // qb),
        in_specs=[cur, prev, cur, nxt, prev, cur, nxt],
        out_specs=(cur, cur),
        compiler_params=_cparams(("parallel", "parallel", "parallel")),
        name=f"dilated_attention_g{g}",
    )(q, k, k, k, v, v, v)


GDN_TT = 512
N_CHAIN = 2 * H_GDN
W_CHAIN = N_CHAIN * HEAD_DIM
CHAIN_TILE = 256


def _head_ones(width):
    idx = np.arange(width) // HEAD_DIM
    return jnp.asarray((idx[:, None] == idx[None, :]).astype(np.float32))


def _gdn_prep_kernel(seq_tiles, x_ref, xp_ref, xn_ref, ba_ref, cw_ref, alog_ref, dtb_ref, ones_ref, eb_ref,
                     q_ref, k_ref, v_ref, beta_ref, gc_ref):
    i = pl.program_id(1)
    x = x_ref[...]
    tt = x.shape[0]
    row = lax.broadcasted_iota(I32, (tt, 1), 0)
    prev = jnp.where(i > 0, xp_ref[...], 0.0)
    nxt = jnp.where(i < seq_tiles - 1, xn_ref[...], 0.0)
    acc = x * cw_ref[CONV_K // 2:CONV_K // 2 + 1, :]
    for d in (1, 2):
        dn = pltpu.roll(x, d, 0)
        for e in range(d):
            dn = jnp.where(row == e, prev[8 - d + e:8 - d + e + 1, :], dn)
        acc = acc + dn * cw_ref[CONV_K // 2 - d:CONV_K // 2 - d + 1, :]
        up = pltpu.roll(x, tt - d, 0)
        for e in range(d):
            up = jnp.where(row == tt - d + e, nxt[e:e + 1, :], up)
        acc = acc + up * cw_ref[CONV_K // 2 + d:CONV_K // 2 + d + 1, :]
    act = acc * _sigmoid(acc)
    q, k, v = act[:, :W_GDN], act[:, W_GDN:2 * W_GDN], act[:, 2 * W_GDN:]
    ones = ones_ref[...]

    def head_sumsq(t):
        return sum(_dot(piece, ones) for piece in _split_bf16(t * t, 2))

    q_ref[...] = q * lax.rsqrt(head_sumsq(q) + NORM_EPS) * (HEAD_DIM ** -0.5)
    k_ref[...] = k * lax.rsqrt(head_sumsq(k) + NORM_EPS)
    v_ref[...] = v
    ba = ba_ref[...]
    beta = _sigmoid(ba)
    z = ba + dtb_ref[...]
    g = -jnp.exp(alog_ref[...]) * (jnp.maximum(z, 0.0) + jnp.log1p(jnp.exp(-jnp.abs(z))))
    lane = lax.broadcasted_iota(I32, (1, LANE), 1)
    bg = jnp.where(lane < N_CHAIN, beta, g)
    eb = eb_ref[...]
    expd = sum(_dot(piece, eb) for piece in _split_bf16(bg, 3))
    beta_ref[...] = expd[:, :W_CHAIN]
    pos = row % GDN_CHUNK
    gc_f = expd[:, W_CHAIN:W_CHAIN + W_GDN]
    gc_b = expd[:, W_CHAIN + W_GDN:]
    sh = 1
    while sh < GDN_CHUNK:
        gc_f = gc_f + jnp.where(pos >= sh, pltpu.roll(gc_f, sh, 0), 0.0)
        gc_b = gc_b + jnp.where(pos < GDN_CHUNK - sh, pltpu.roll(gc_b, tt - sh, 0), 0.0)
        sh *= 2
    gc_ref[...] = jnp.concatenate([gc_f, gc_b], axis=1)


def _gdn_prep(proj3, conv_w, a_log, dt_bias):
    b, s, _ = proj3.shape
    tt = min(GDN_TT, s)
    nt = s // tt
    w3 = 3 * W_GDN
    cw = jnp.zeros((8, w3), F32).at[:CONV_K].set(conv_w)
    alog_row = jnp.zeros((1, LANE), F32).at[0, N_CHAIN:2 * N_CHAIN].set(a_log.reshape(-1))
    dtb_row = jnp.zeros((1, LANE), F32).at[0, N_CHAIN:2 * N_CHAIN].set(dt_bias.reshape(-1))
    eb = np.zeros((LANE, 2 * W_CHAIN), np.float32)
    for c in range(N_CHAIN):
        eb[c, c * HEAD_DIM:(c + 1) * HEAD_DIM] = 1.0
        eb[N_CHAIN + c, W_CHAIN + c * HEAD_DIM:W_CHAIN + (c + 1) * HEAD_DIM] = 1.0
    blk8 = tt // 8
    out_w = lambda w: jax.ShapeDtypeStruct((b, s, w), F32)
    const = lambda a: pl.BlockSpec(a.shape, lambda bi, i: (0,) * a.ndim)
    ones = _head_ones(W_GDN)
    eb = jnp.asarray(eb)
    ospec = lambda w: pl.BlockSpec((None, tt, w), lambda bi, i: (bi, i, 0))
    return pl.pallas_call(
        functools.partial(_gdn_prep_kernel, nt),
        out_shape=(out_w(W_GDN), out_w(W_GDN), out_w(W_GDN), out_w(W_CHAIN), out_w(W_CHAIN)),
        grid=(b, nt),
        in_specs=[pl.BlockSpec((None, tt, w3), lambda bi, i: (bi, i, 0)),
                  pl.BlockSpec((None, 8, w3), lambda bi, i: (bi, jnp.maximum(i * blk8 - 1, 0), 0)),
                  pl.BlockSpec((None, 8, w3), lambda bi, i: (bi, jnp.minimum((i + 1) * blk8, s // 8 - 1), 0)),
                  pl.BlockSpec((None, tt, LANE), lambda bi, i: (bi, i, C_BA // LANE)),
                  const(cw), const(alog_row), const(dtb_row), const(ones), const(eb)],
        out_specs=(ospec(W_GDN), ospec(W_GDN), ospec(W_GDN), ospec(W_CHAIN), ospec(W_CHAIN)),
        compiler_params=_cparams(("parallel", "parallel")),
        name="gdn_prep",
    )(proj3, proj3, proj3, proj3, cw, alog_row, dtb_row, ones, eb)


GDN_CS = 4


def _gdn_consts():
    c = GDN_CHUNK
    i = np.arange(c)[:, None]
    lane = np.arange(W_CHAIN)[None, :]
    j = lane % HEAD_DIM
    fwd = lane < W_GDN
    incl = np.where(fwd, j <= i, j >= i)
    strict = np.where(fwd, j < i, j > i)
    eye = (j == i)
    masks = np.stack([incl, strict, eye]).astype(np.float32)
    r = np.arange(CHAIN_TILE)[:, None] // HEAD_DIM
    cidx = np.arange(CHAIN_TILE)[None, :] // HEAD_DIM
    bd = (r == cidx).astype(np.float32)
    return jnp.asarray(masks), jnp.asarray(bd)


def _gdn_kernel(qf_ref, kf_ref, vf_ref, qb_ref, kb_ref, vb_ref, betaf_ref, betab_ref, gcf_ref, gcb_ref,
                masks_ref, bd_ref, of_ref, ob_ref, state_ref):
    @pl.when(pl.program_id(1) == 0)
    def _():
        state_ref[...] = jnp.zeros_like(state_ref)

    c = GDN_CHUNK
    ntile = W_CHAIN // CHAIN_TILE
    incl = masks_ref[0]
    strict = masks_ref[1]
    eye = masks_ref[2]
    bd = bd_ref[...]
    bd16 = bd.astype(BF16)
    reps = CHAIN_TILE // HEAD_DIM

    def bdiag(r_t):
        return jnp.concatenate([r_t.astype(BF16)] * reps, axis=0) * bd16

    def chain_mm(lhs, rhs):
        outs = []
        for t in range(ntile):
            sl = slice(t * CHAIN_TILE, (t + 1) * CHAIN_TILE)
            outs.append(_dot(lhs[:, sl], bdiag(rhs[:, sl])))
        return jnp.concatenate(outs, axis=1)

    for step in range(GDN_CS):
        fs = slice(step * c, (step + 1) * c)
        bs = slice((GDN_CS - 1 - step) * c, (GDN_CS - step) * c)
        cat = lambda a, bb: jnp.concatenate([a, bb], axis=1)
        q = cat(qf_ref[fs, :], qb_ref[bs, :])
        k = cat(kf_ref[fs, :], kb_ref[bs, :])
        v = cat(vf_ref[fs, :], vb_ref[bs, :])
        beta = cat(betaf_ref[fs, :W_GDN], betab_ref[bs, W_GDN:])
        gcol = cat(gcf_ref[fs, :W_GDN], gcb_ref[bs, W_GDN:])
        grow = jnp.sum(gcol * eye, axis=0, keepdims=True)
        glast = cat(gcol[c - 1:c, :W_GDN], gcol[0:1, W_GDN:])
        decay = jnp.where(incl > 0, jnp.exp(jnp.where(incl > 0, gcol - grow, 0.0)), 0.0)
        kb = k * beta
        lhs = jnp.concatenate([kb, q], axis=0)
        sc = []
        for t in range(ntile):
            sl = slice(t * CHAIN_TILE, (t + 1) * CHAIN_TILE)
            sc.append(_dot(lhs[:, sl], bdiag(k[:, sl]), NT))
        sc = jnp.concatenate(sc, axis=1)
        a_mat = sc[:c] * decay * strict
        intra = sc[c:] * decay
        p = eye - a_mat
        x = chain_mm(a_mat, a_mat)
        for it in range(5):
            if it < 4:
                both = chain_mm(jnp.concatenate([p, x], axis=0), x)
                p = p + both[:c]
                x = both[c:]
            else:
                p = p + chain_mm(p, x)
        egc = jnp.exp(gcol)
        w = chain_mm(p, kb * egc)
        u = chain_mm(p, v * beta)
        q_dec = q * egc
        k_dec = k * jnp.exp(glast - gcol)
        g_last = jnp.exp(glast)
        lhs2 = jnp.concatenate([w, q_dec], axis=0)
        ws_qs = jnp.concatenate(
            [_dot(lhs2[:, t * CHAIN_TILE:(t + 1) * CHAIN_TILE], state_ref[t]) for t in range(ntile)], axis=1)
        v_new = u - ws_qs[:c]
        o = ws_qs[c:] + chain_mm(intra, v_new)
        for t in range(ntile):
            sl = slice(t * CHAIN_TILE, (t + 1) * CHAIN_TILE)
            state_ref[t] = state_ref[t] * g_last[:, sl] + _dot(k_dec[:, sl], v_new[:, sl], TN) * bd
        of_ref[fs, :] = o[:, :W_GDN]
        ob_ref[bs, :] = o[:, W_GDN:]


def _gdn_main(qn, kn, vv, beta, gc):
    b, s, _ = qn.shape
    ts = GDN_CS * GDN_CHUNK
    nb = s // ts
    masks, bd = _gdn_consts()
    fwd = lambda w: pl.BlockSpec((None, ts, w), lambda bi, i: (bi, i, 0))
    bwd = lambda w: pl.BlockSpec((None, ts, w), lambda bi, i: (bi, nb - 1 - i, 0))
    const = lambda a: pl.BlockSpec(a.shape, lambda bi, i: (0,) * a.ndim)
    out = jax.ShapeDtypeStruct((b, s, W_GDN), F32)
    return pl.pallas_call(
        _gdn_kernel,
        out_shape=(out, out),
        grid=(b, nb),
        in_specs=[fwd(W_GDN), fwd(W_GDN), fwd(W_GDN), bwd(W_GDN), bwd(W_GDN), bwd(W_GDN),
                  fwd(W_CHAIN), bwd(W_CHAIN), fwd(W_CHAIN), bwd(W_CHAIN), const(masks), const(bd)],
        out_specs=(fwd(W_GDN), bwd(W_GDN)),
        scratch_shapes=[pltpu.VMEM((W_CHAIN // CHAIN_TILE, CHAIN_TILE, CHAIN_TILE), F32)],
        compiler_params=_cparams(("parallel", "arbitrary")),
        name="gdn_scan",
    )(qn, kn, vv, qn, kn, vv, beta, beta, gc, gc, masks, bd)


TILE_ROWS = D_MODEL // LANE


def _store_token_tiles(ref, val):
    t = val.shape[0]
    for j in range(TILE_ROWS):
        ref[pl.ds(j, t, stride=TILE_ROWS), :] = val[:, j * LANE:(j + 1) * LANE]


def _load_token_tiles(ref, t):
    return jnp.concatenate([ref[pl.ds(j, t, stride=TILE_ROWS), :] for j in range(TILE_ROWS)], axis=1)


def _layer_norm(y, g, b):
    mu = jnp.mean(y, axis=-1, keepdims=True)
    d = y - mu
    var = jnp.mean(d * d, axis=-1, keepdims=True)
    return d * lax.rsqrt(var + LN_EPS) * g + b


def _split_bf16(a, terms):
    pieces, rest = [], a
    for _ in range(terms):
        piece = rest.astype(BF16)
        pieces.append(piece)
        rest = rest - piece.astype(F32)
    return pieces


def _merge_kernel(x_ref, gates_ref, na_ref, o0, s0, o1, s1, o2, s2, of_ref, ob_ref, z_ref,
                  wna_ref, wdil_ref, wgdn_ref, wout_ref, normw_ref, ones_ref, g_ref, b_ref, wrh_ref, wrl_ref,
                  x1_ref, xa_ref, aff_ref, cls_ref):
    vals = []
    for slot, ref in enumerate((o0, s0, o1, s1, o2, s2)):
        if len(ref.shape) == 2:
            vals.append(ref[...])
        else:
            dil, rows, _ = ref.shape
            for c in range(dil):
                cls_ref[slot, pl.ds(c, rows, stride=dil), :] = ref[c]
            vals.append(cls_ref[slot])
    lses = vals[1::2]
    mm = jnp.maximum(jnp.maximum(lses[0], lses[1]), lses[2])
    wts = [jnp.exp(l - mm) for l in lses]
    o_dil = (wts[0] * vals[0] + wts[1] * vals[2] + wts[2] * vals[4]) / (wts[0] + wts[1] + wts[2])
    o = of_ref[...] + ob_ref[...]
    ones = ones_ref[...]
    ms = sum(_dot(piece, ones) for piece in _split_bf16(o * o, 2)) * (1.0 / HEAD_DIM)
    z = z_ref[...]
    o_gdn = o * lax.rsqrt(ms + NORM_EPS) * normw_ref[...] * (z * _sigmoid(z))
    br_na = _dot(na_ref[...], wna_ref[...])
    br_dil = _dot(o_dil, wdil_ref[...])
    br_gdn = _dot(o_gdn, wgdn_ref[...])
    gate = _sigmoid(gates_ref[...])
    merged = (gate[:, :D_MODEL] * br_na + gate[:, D_MODEL:2 * D_MODEL] * br_dil
              + gate[:, 2 * D_MODEL:] * br_gdn)
    mix = _dot(merged, wout_ref[...])
    x1 = _layer_norm(DN_ALPHA * x_ref[...] + mix, g_ref[...], b_ref[...])
    _store_token_tiles(x1_ref, x1)
    _store_token_tiles(xa_ref, DN_ALPHA * x1)
    xh, xl = _split_bf16(x1, 2)
    logits = _dot(xh, wrh_ref[...]) + _dot(xl, wrh_ref[...]) + _dot(xh, wrl_ref[...])
    e = jnp.exp(logits - jnp.max(logits, axis=-1, keepdims=True))
    aff_ref[...] = e / jnp.sum(e, axis=-1, keepdims=True)


def _merge(x, proj, o_na, dil, o_f, o_b, b, s, w_br_na, w_br_dil, w_br_gdn, w_out, norm_w, ln_g, ln_b, w_router,
           tm=256):
    n = b * s
    tm = min(tm, s)
    nt = s // tm
    tok = lambda w, c=0: pl.BlockSpec((tm, w), lambda bi, i: (bi * nt + i, c))
    const = lambda a: pl.BlockSpec(a.shape, lambda bi, i: (0,) * a.ndim)
    normw = jnp.tile(norm_w, H_GDN).reshape(1, W_GDN)
    ones = _head_ones(W_GDN)
    wr_hi = w_router.astype(BF16)
    wr_lo = (w_router - wr_hi.astype(F32)).astype(BF16)
    consts = [w_br_na.astype(BF16), w_br_dil.astype(BF16), w_br_gdn.astype(BF16), w_out.astype(BF16),
              normw, ones, ln_g.reshape(1, -1), ln_b.reshape(1, -1), wr_hi, wr_lo]
    dil_flat, dil_specs = [], []
    for (_, d), grp in zip(DIL_PATTERN, dil):
        for t in grp:
            dil_flat.append(t)
            if d == 1:
                dil_specs.append(pl.BlockSpec((None, None, tm, LANE), lambda bi, i: (bi, 0, i, 0)))
            else:
                dil_specs.append(pl.BlockSpec((None, d, tm // d, LANE), lambda bi, i: (bi, 0, i, 0)))
    big = jax.ShapeDtypeStruct((n * TILE_ROWS, LANE), F32)
    tile_spec = pl.BlockSpec((tm * TILE_ROWS, LANE), lambda bi, i: (bi * nt + i, 0))
    return pl.pallas_call(
        _merge_kernel,
        out_shape=(big, big, jax.ShapeDtypeStruct((n, N_EXPERTS), F32)),
        grid=(b, nt),
        in_specs=[tok(D_MODEL), tok(N_BRANCH * D_MODEL, C_GATE // (N_BRANCH * D_MODEL)), tok(W_NA)]
                 + dil_specs + [tok(W_GDN), tok(W_GDN), tok(W_GDN, C_Z // W_GDN)]
                 + [const(a) for a in consts],
        out_specs=(tile_spec, tile_spec, tok(N_EXPERTS)),
        scratch_shapes=[pltpu.VMEM((2 * len(DIL_PATTERN), tm, LANE), F32)],
        compiler_params=_cparams(("parallel", "parallel")),
        name="merge_ln1_router",
    )(x, proj, o_na, *dil_flat, o_f, o_b, proj, *consts)


def _thresh_kernel(cap, aff_ref, tau_ref, cgt_ref):
    bits = pltpu.bitcast(aff_ref[...], I32)
    ne = bits.shape[0]

    def count_ge(t):
        return jnp.sum((bits >= t).astype(I32), axis=1, keepdims=True)

    def body(_, carry):
        lo, hi = carry
        mid = lo + jnp.right_shift(hi - lo, 1)
        ok = count_ge(mid) >= cap
        return jnp.where(ok, mid, lo), jnp.where(ok, hi, mid)

    lo0 = jnp.zeros((ne, 1), I32)
    hi0 = jnp.full((ne, 1), 0x7F800000, I32)
    lo, _ = lax.fori_loop(0, 31, body, (lo0, hi0))
    tau_ref[...] = jnp.broadcast_to(lo, tau_ref.shape)
    cgt_ref[...] = jnp.broadcast_to(jnp.sum((bits > lo).astype(I32), axis=1, keepdims=True), cgt_ref.shape)


def _thresholds(aff_t, cap):
    ne, n = aff_t.shape
    out = jax.ShapeDtypeStruct((ne, LANE), I32)
    return pl.pallas_call(
        functools.partial(_thresh_kernel, cap),
        out_shape=(out, out),
        in_specs=[pl.BlockSpec((ne, n), lambda: (0, 0))],
        out_specs=(pl.BlockSpec((ne, LANE), lambda: (0, 0)),) * 2,
        compiler_params=pltpu.CompilerParams(vmem_limit_bytes=VMEM_LIMIT),
        name="expert_thresholds",
    )(aff_t)


CMP_TC = 128
CMP_WIN = CMP_TC // LANE + 1


def _compact_kernel(aff_ref, tau_ref, take_ref, tri_ref, idx_ref, gate_ref, carry_ref):
    step = pl.program_id(0)

    @pl.when(step == 0)
    def _():
        idx_ref[...] = jnp.zeros_like(idx_ref)
        gate_ref[...] = jnp.zeros_like(gate_ref)
        carry_ref[...] = jnp.zeros_like(carry_ref)

    aff = aff_ref[...]
    tc = aff.shape[0]
    bits = pltpu.bitcast(aff, I32)
    tau = tau_ref[...]
    gt = (bits > tau).astype(F32)
    eq = (bits == tau).astype(F32)
    tri = tri_ref[...]
    carry = carry_ref[...]
    tie_incl = _dot(tri, eq) + carry[1:2, :]
    take = eq * ((tie_incl - eq) < take_ref[...].astype(F32)).astype(F32)
    sel = gt + take
    pos_incl = _dot(tri, sel) + carry[0:1, :]
    pos = pos_incl - sel
    carry_ref[0:1, :] = pos_incl[tc - 1:tc, :]
    carry_ref[1:2, :] = tie_incl[tc - 1:tc, :]
    tok = step * tc + lax.broadcasted_iota(I32, (tc, 1), 0)
    lane = lax.broadcasted_iota(I32, (1, LANE), 1)
    for e in range(N_EXPERTS):
        start = carry[0, e].astype(I32)
        r0 = start // LANE
        rel = pos[:, e:e + 1].astype(I32) - r0 * LANE
        sel_e = sel[:, e:e + 1] > 0
        aff_e = aff[:, e:e + 1]
        for j in range(CMP_WIN):
            hit = (rel == lane + j * LANE) & sel_e
            idx_ref[e, r0 + j] += jnp.sum(jnp.where(hit, tok, 0), axis=0, keepdims=True)
            gate_ref[e, r0 + j] += jnp.sum(jnp.where(hit, aff_e, 0.0), axis=0, keepdims=True)


def _compact(aff, tau_row, take_row, cap):
    n, ne = aff.shape
    tc = min(CMP_TC, n)
    rows = cap // LANE + CMP_WIN
    t = np.arange(tc)
    tri = jnp.asarray((t[None, :] <= t[:, None]).astype(np.float32))
    idx, gate = pl.pallas_call(
        _compact_kernel,
        out_shape=(jax.ShapeDtypeStruct((ne, rows, 1, LANE), I32),
                   jax.ShapeDtypeStruct((ne, rows, 1, LANE), F32)),
        grid=(n // tc,),
        in_specs=[pl.BlockSpec((tc, ne), lambda i: (i, 0)),
                  pl.BlockSpec((1, ne), lambda i: (0, 0)),
                  pl.BlockSpec((1, ne), lambda i: (0, 0)),
                  pl.BlockSpec((tc, tc), lambda i: (0, 0))],
        out_specs=(pl.BlockSpec((ne, rows, 1, LANE), lambda i: (0, 0, 0, 0)),) * 2,
        scratch_shapes=[pltpu.VMEM((2, ne), F32)],
        compiler_params=_cparams(("arbitrary",)),
        name="expert_compaction",
    )(aff, tau_row, take_row, tri)
    idx = idx[:, :cap // LANE].reshape(ne, cap)
    gate = gate[:, :cap // LANE].reshape(ne, cap)
    return idx, gate


MOE_TM = 256


def _moe_kernel(idx_ref, idxn_ref, gate_ref, x_hbm, acc_in, wg_ref, wu_ref, wd_ref, acc_hbm, xbuf, ybuf, sems):
    del acc_in
    nk = pl.num_programs(1)
    step = pl.program_id(0) * nk + pl.program_id(1)
    last = pl.num_programs(0) * nk - 1
    slot = step % 2
    tm = gate_ref.shape[0]
    sem_x, sem_y, sem_s = 0, 2, 3

    rows = tm * TILE_ROWS

    def token(hbm, ref, r):
        return hbm.at[pl.ds(pl.multiple_of(ref[0, r] * TILE_ROWS, TILE_ROWS), TILE_ROWS), :]

    def tile(r):
        return pl.ds(r * TILE_ROWS, TILE_ROWS)

    def x_row(ref, r, sl):
        return pltpu.make_async_copy(token(x_hbm, ref, r), xbuf.at[sl, tile(r), :], sems.at[sem_x + sl])

    def x_tile_wait(sl):
        pltpu.make_async_copy(x_hbm.at[pl.ds(0, rows), :], xbuf.at[sl], sems.at[sem_x + sl]).wait()

    def scatter_wait():
        pltpu.make_async_copy(ybuf, acc_hbm.at[pl.ds(0, rows), :], sems.at[sem_s]).wait()

    @pl.when(step == 0)
    def _():
        for r in range(tm):
            x_row(idx_ref, r, 0).start()

    x_tile_wait(slot)
    for r in range(tm):
        x_row(idxn_ref, r, 1 - slot).start()
    xb = _load_token_tiles(xbuf.at[slot], tm).astype(BF16)
    hg = jnp.dot(xb, wg_ref[...], preferred_element_type=F32)
    hu = jnp.dot(xb, wu_ref[...], preferred_element_type=F32)
    hid = (hg * _sigmoid(hg) * hu).astype(BF16)

    @pl.when(step > 0)
    def _():
        scatter_wait()

    for r in range(tm):
        pltpu.make_async_copy(token(acc_hbm, idx_ref, r), ybuf.at[tile(r), :], sems.at[sem_y]).start()
    ye = jnp.dot(hid, wd_ref[...], preferred_element_type=F32) * gate_ref[...]
    pltpu.make_async_copy(acc_hbm.at[pl.ds(0, rows), :], ybuf, sems.at[sem_y]).wait()
    _store_token_tiles(ybuf, _load_token_tiles(ybuf, tm) + ye)
    for r in range(tm):
        pltpu.make_async_copy(ybuf.at[tile(r), :], token(acc_hbm, idx_ref, r), sems.at[sem_s]).start()

    @pl.when(step == last)
    def _():
        scatter_wait()
        x_tile_wait(1 - slot)


def _moe(x1, acc, idx, gate, wg, wu, wd):
    d = wg.shape[1]
    ne, cap = idx.shape
    tm = min(MOE_TM, cap)
    nk = cap // tm
    f = wg.shape[2]
    idx3 = idx.reshape(ne, nk, 1, tm)
    gate4 = gate.reshape(ne, nk, tm, 1)

    def nxt(e, k):
        s = jnp.minimum(e * nk + k + 1, ne * nk - 1)
        return (s // nk, s % nk, 0, 0)

    return pl.pallas_call(
        _moe_kernel,
        out_shape=jax.ShapeDtypeStruct(acc.shape, F32),
        grid=(ne, nk),
        in_specs=[pl.BlockSpec((None, None, 1, tm), lambda e, k: (e, k, 0, 0), memory_space=pltpu.SMEM),
                  pl.BlockSpec((None, None, 1, tm), nxt, memory_space=pltpu.SMEM),
                  pl.BlockSpec((None, None, tm, 1), lambda e, k: (e, k, 0, 0)),
                  pl.BlockSpec(memory_space=pl.ANY),
                  pl.BlockSpec(memory_space=pl.ANY),
                  pl.BlockSpec((None, d, f), lambda e, k: (e, 0, 0)),
                  pl.BlockSpec((None, d, f), lambda e, k: (e, 0, 0)),
                  pl.BlockSpec((None, f, d), lambda e, k: (e, 0, 0))],
        out_specs=pl.BlockSpec(memory_space=pl.ANY),
        scratch_shapes=[pltpu.VMEM((2, tm * TILE_ROWS, LANE), F32), pltpu.VMEM((tm * TILE_ROWS, LANE), F32),
                        pltpu.SemaphoreType.DMA((4,))],
        input_output_aliases={4: 0},
        compiler_params=pltpu.CompilerParams(dimension_semantics=("arbitrary", "arbitrary"),
                                             vmem_limit_bytes=VMEM_LIMIT, has_side_effects=True),
        name="expert_ffn",
    )(idx3, idx3, gate4, x1, acc, wg, wu, wd)


def _ln_kernel(x_ref, g_ref, b_ref, o_ref):
    o_ref[...] = _layer_norm(_load_token_tiles(x_ref, o_ref.shape[0]), g_ref[...], b_ref[...])


def _ln(x, g, b, tm=512):
    n, d = x.shape[0] // TILE_ROWS, D_MODEL
    tm = min(tm, n)
    return pl.pallas_call(
        _ln_kernel,
        out_shape=jax.ShapeDtypeStruct((n, d), F32),
        grid=(n // tm,),
        in_specs=[pl.BlockSpec((tm * TILE_ROWS, LANE), lambda i: (i, 0)), pl.BlockSpec((1, d), lambda i: (0, 0)),
                  pl.BlockSpec((1, d), lambda i: (0, 0))],
        out_specs=pl.BlockSpec((tm, d), lambda i: (i, 0)),
        compiler_params=_cparams(("parallel",)),
        name="layer_norm2",
    )(x, g.reshape(1, d), b.reshape(1, d))


def _token_mixer_parts(x, b, s, w_in_p, bias_tab, conv_w, a_log, dt_bias, rope_tabs):
    n = b * s
    proj = _inproj(x, w_in_p)
    proj3 = proj.reshape(b, s, D_PAD)
    o_na = _na_attention(proj3, bias_tab).reshape(n, W_NA)
    dil = [_dil_group(*qkv, g) for g, qkv in enumerate(_rope(proj3, *rope_tabs))]
    qn, kn, vv, beta, gc = _gdn_prep(proj3, conv_w, a_log, dt_bias)
    o_f, o_b = _gdn_main(qn, kn, vv, beta, gc)
    return proj, o_na, dil, o_f.reshape(n, W_GDN), o_b.reshape(n, W_GDN)


def _expert_choice(x1, xa, aff, wg, wu, wd):
    n = aff.shape[0]
    cap = EC_FACTOR * n // N_EXPERTS
    tau, cgt = _thresholds(aff.T, cap)
    tau_row = tau[:, 0].reshape(1, N_EXPERTS)
    take_row = (cap - cgt[:, 0]).reshape(1, N_EXPERTS)
    idx, gate = _compact(aff, tau_row, take_row, cap)
    return _moe(x1, xa, idx, gate, wg, wu, wd)


def _trunk(x3, params, shared):
    b, s, d = x3.shape
    n = b * s
    x = x3.reshape(n, d)
    rope_tabs = _rope_tables(s)
    for l in range(DEPTH):
        p = {k: v[l] for k, v in params.items()}
        sh = shared[l]
        proj, o_na, dil, o_f, o_b = _token_mixer_parts(x, b, s, sh["w_in"], sh["bias_tab"], p["conv_w"],
                                                       p["a_log"], p["dt_bias"], rope_tabs)
        x1, xa, aff = _merge(x, proj, o_na, dil, o_f, o_b, b, s, p["w_br_na"], p["w_br_dil"], p["w_br_gdn"],
                             p["w_out"], p["gdn_norm_w"], p["ln1_g"], p["ln1_b"], p["w_router"])
        acc = _expert_choice(x1, xa, aff, sh["w_gate"], sh["w_up"], sh["w_down"])
        x = _ln(acc, p["ln2_g"], p["ln2_b"])
    return x.reshape(b, s, d)


def kernel(x_prompt, x_sample, w_in, na_rpb, conv_w, a_log, dt_bias, gdn_norm_w, w_br_na, w_br_dil, w_br_gdn,
           w_out, ln1_g, ln1_b, w_router, w_up, w_gate, w_down, ln2_g, ln2_b):
    params = dict(conv_w=conv_w, a_log=a_log, dt_bias=dt_bias, gdn_norm_w=gdn_norm_w, w_br_na=w_br_na,
                  w_br_dil=w_br_dil, w_br_gdn=w_br_gdn, w_out=w_out, ln1_g=ln1_g, ln1_b=ln1_b,
                  w_router=w_router, ln2_g=ln2_g, ln2_b=ln2_b)
    shared = [dict(w_in=_reorder_w_in(w_in[l]), bias_tab=_na_bias_table(na_rpb[l]),
                   w_gate=w_gate[l].astype(BF16), w_up=w_up[l].astype(BF16), w_down=w_down[l].astype(BF16))
              for l in range(DEPTH)]
    return _trunk(x_prompt, params, shared), _trunk(x_sample, params, shared)
```

```python
import functools
import math

import numpy as np
import jax
import jax.numpy as jnp
from jax import lax
from jax.experimental import pallas as pl
from jax.experimental.pallas import tpu as pltpu

F32 = jnp.float32
BF16 = jnp.bfloat16
I32 = jnp.int32

D_MODEL = 1024
DEPTH = 2
HEAD_DIM = 64
H_NA = 4
H_DIL = 6
H_GDN = 6
W_NA = H_NA * HEAD_DIM
W_DIL = H_DIL * HEAD_DIM
W_GDN = H_GDN * HEAD_DIM
GRID_W = 64
NA_KH = 8
NA_KW = 16
DIL_PATTERN = ((128, 1), (512, 4), (2048, 16))
DIL_HPG = H_DIL // len(DIL_PATTERN)
W_DIL_OUT = DIL_HPG * HEAD_DIM
ROPE_THETA = 10000.0
GDN_CHUNK = 64
CONV_K = 5
N_EXPERTS = 16
D_EXPERT = 2048
EC_FACTOR = 2
N_BRANCH = 3
DN_ALPHA = (2 * DEPTH) ** 0.25
LN_EPS = 1e-5
NORM_EPS = 1e-6
NEG_BIG = -1e30

C_GDN = 0
C_DIL = C_GDN + 3 * W_GDN
C_NA = C_DIL + 3 * W_DIL
C_GATE = C_NA + 3 * W_NA
C_Z = C_GATE + N_BRANCH * D_MODEL
C_BA = C_Z + W_GDN
LANE = 128
VMEM_LIMIT = 56 * 1024 * 1024


def _cparams(sem):
    return pltpu.CompilerParams(dimension_semantics=sem, vmem_limit_bytes=VMEM_LIMIT)


def _dot(a, b, dims=None, hi=False):
    if dims is None:
        dims = (((a.ndim - 1,), (0,)), ((), ()))
    if hi:
        return lax.dot_general(a.astype(F32), b.astype(F32), dims, precision=lax.Precision.HIGHEST,
                               preferred_element_type=F32)
    return lax.dot_general(a.astype(BF16), b.astype(BF16), dims, preferred_element_type=F32)


NT = (((1,), (1,)), ((), ()))
TN = (((0,), (0,)), ((), ()))


def _sigmoid(x):
    return 1.0 / (1.0 + jnp.exp(-x))


def _inproj_kernel(x_ref, w_ref, wba_ref, o_ref, ba_ref, xb_ref):
    @pl.when(pl.program_id(1) == 0)
    def _():
        xb_ref[...] = x_ref[...].astype(BF16)
        ba_ref[...] = jnp.dot(xb_ref[...], wba_ref[...], preferred_element_type=F32)

    o_ref[...] = jnp.dot(xb_ref[...], w_ref[...], preferred_element_type=F32).astype(BF16)


def _inproj(x, w_main, w_ba, tm=1024, tn=W_GDN):
    n, k = x.shape
    tm = min(tm, n)
    return pl.pallas_call(
        _inproj_kernel,
        out_shape=(jax.ShapeDtypeStruct((n, C_BA), BF16), jax.ShapeDtypeStruct((n, LANE), F32)),
        grid=(n // tm, C_BA // tn),
        in_specs=[pl.BlockSpec((tm, k), lambda i, j: (i, 0)),
                  pl.BlockSpec((k, tn), lambda i, j: (0, j)),
                  pl.BlockSpec((k, LANE), lambda i, j: (0, 0))],
        out_specs=(pl.BlockSpec((tm, tn), lambda i, j: (i, j)), pl.BlockSpec((tm, LANE), lambda i, j: (i, 0))),
        scratch_shapes=[pltpu.VMEM((tm, k), BF16)],
        compiler_params=_cparams(("parallel", "arbitrary")),
        name="inproj",
    )(x, w_main, w_ba)


def _reorder_w_in(w):
    o_qd = 3 * W_NA
    o_qc = o_qd + 3 * W_DIL
    o_z = o_qc + 3 * W_GDN
    o_b = o_z + W_GDN
    o_gate = o_b + 4 * H_GDN
    main = jnp.concatenate([w[:, o_qc:o_z], w[:, o_qd:o_qc], w[:, 0:o_qd], w[:, o_gate:], w[:, o_z:o_b]], axis=1)
    ba = jnp.concatenate([w[:, o_b:o_gate], jnp.zeros((w.shape[0], LANE - 4 * H_GDN), w.dtype)], axis=1)
    return main.astype(BF16), ba.astype(BF16)


NA_RQ = 8
NA_WIN_ROWS = 16
NA_KB = 256


def _toeplitz_kernel(rpb_ref, onehot_ref, o_ref):
    o_ref[...] = _dot(rpb_ref[...], onehot_ref[...], hi=True)


def _na_bias_table(rpb):
    h, ndr, ndc = rpb.shape
    qc = np.arange(GRID_W)[:, None]
    kc = np.arange(GRID_W)[None, :]
    dc = np.clip(kc - qc, -(NA_KW - 1), NA_KW - 1) + (NA_KW - 1)
    onehot = (np.arange(LANE)[:, None, None] == dc[None]).astype(np.float32)
    onehot = jnp.asarray(onehot.reshape(LANE, GRID_W * GRID_W))
    rows = -(-h * ndr // 8) * 8
    rpb2 = jnp.zeros((rows, LANE), F32).at[:h * ndr, :ndc].set(rpb.reshape(h * ndr, ndc))
    toe = pl.pallas_call(
        _toeplitz_kernel,
        out_shape=jax.ShapeDtypeStruct((rows, GRID_W * GRID_W), F32),
        name="na_bias_toeplitz",
    )(rpb2, onehot)
    toe = toe[:h * ndr].reshape(h, ndr, GRID_W, GRID_W)
    ws = np.clip(qc - NA_KW // 2, 0, GRID_W - NA_KW)
    ok = jnp.asarray((kc >= ws) & (kc < ws + NA_KW))[None, :, None, :]
    tabs = []
    for off in range(NA_KH):
        lo = NA_KH - 1 - off
        t = jnp.transpose(toe[:, lo:lo + NA_KH], (0, 2, 1, 3))
        tabs.append(jnp.where(ok, t, NEG_BIG).reshape(h * GRID_W, NA_KH * GRID_W))
    return jnp.stack(tabs)


def _na_kernel(rows, q_ref, k0, k1, k2, k3, v0, v1, v2, v3, bias_ref, o_ref, kc_ref, vc_ref):
    rq = pl.program_id(1)
    for j, (kr, vr) in enumerate(((k0, v0), (k1, v1), (k2, v2), (k3, v3))):
        kc_ref[j * NA_KB:(j + 1) * NA_KB, :] = kr[...].astype(BF16)
        vc_ref[j * NA_KB:(j + 1) * NA_KB, :] = vr[...].astype(BF16)
    ws = jnp.clip(rq * NA_RQ - NA_KH // 2, 0, rows - NA_WIN_ROWS)
    lane = lax.broadcasted_iota(I32, (1, W_NA), 1)
    masks = [(lane // HEAD_DIM == h).astype(F32) for h in range(H_NA)]
    for j in range(NA_RQ):
        r = rq * NA_RQ + j
        r0 = jnp.clip(r - NA_KH // 2, 0, rows - NA_KH)
        koff = pl.multiple_of((r0 - ws) * GRID_W, GRID_W)
        off = r - r0
        qj = q_ref[j * GRID_W:(j + 1) * GRID_W, :] * (HEAD_DIM ** -0.5)
        qs = jnp.concatenate([qj * m for m in masks], axis=0).astype(BF16)
        kw = kc_ref[pl.ds(koff, NA_KH * GRID_W), :]
        vw = vc_ref[pl.ds(koff, NA_KH * GRID_W), :]
        s = _dot(qs, kw, NT) + bias_ref[off]
        m = jnp.max(s, axis=-1, keepdims=True)
        p = jnp.exp(s - m)
        den = jnp.sum(p, axis=-1, keepdims=True)
        o = _dot(p, vw) / den
        acc = o[0:GRID_W] * masks[0]
        for h in range(1, H_NA):
            acc = acc + o[h * GRID_W:(h + 1) * GRID_W] * masks[h]
        o_ref[j * GRID_W:(j + 1) * GRID_W, :] = acc


def _na_attention(proj3, bias_tab):
    b, s, _ = proj3.shape
    rows = s // GRID_W
    nrq = rows // NA_RQ
    tq = NA_RQ * GRID_W
    cq, ck, cv = C_NA // W_NA, C_NA // W_NA + 1, C_NA // W_NA + 2
    nkb = s // NA_KB

    def kmap(j, col):
        def f(bi, rq):
            kb0 = jnp.clip(2 * rq - 1, 0, nkb - NA_WIN_ROWS * GRID_W // NA_KB)
            return (bi, kb0 + j, col)
        return f

    in_specs = [pl.BlockSpec((None, tq, W_NA), lambda bi, rq: (bi, rq, cq))]
    in_specs += [pl.BlockSpec((None, NA_KB, W_NA), kmap(j, ck)) for j in range(4)]
    in_specs += [pl.BlockSpec((None, NA_KB, W_NA), kmap(j, cv)) for j in range(4)]
    in_specs += [pl.BlockSpec(bias_tab.shape, lambda bi, rq: (0, 0, 0))]
    return pl.pallas_call(
        functools.partial(_na_kernel, rows),
        out_shape=jax.ShapeDtypeStruct((b, s, W_NA), F32),
        grid=(b, nrq),
        in_specs=in_specs,
        out_specs=pl.BlockSpec((None, tq, W_NA), lambda bi, rq: (bi, rq, 0)),
        scratch_shapes=[pltpu.VMEM((NA_WIN_ROWS * GRID_W, W_NA), BF16),
                        pltpu.VMEM((NA_WIN_ROWS * GRID_W, W_NA), BF16)],
        compiler_params=_cparams(("parallel", "parallel")),
        name="na_attention",
    )(proj3, *([proj3] * 8), bias_tab)


def _rope_tables(s):
    half = HEAD_DIM // 2
    inv = ROPE_THETA ** (-jnp.arange(half, dtype=F32) / half)
    ang = jnp.arange(s, dtype=F32)[:, None] * inv[None, :]
    cos, sin = jnp.cos(ang), jnp.sin(ang)
    cos_t = jnp.concatenate([cos, cos, cos, cos], axis=1)
    sin_t = jnp.concatenate([-sin, sin, -sin, sin], axis=1)
    return cos_t, sin_t


def _rope_kernel(q_ref, k_ref, v_ref, cos_ref, sin_ref, *refs):
    ngrp = len(DIL_PATTERN)
    outs, stage_ref = refs[:3 * ngrp], refs[3 * ngrp]
    tt = q_ref.shape[0]
    cos = jnp.concatenate([cos_ref[...]] * ngrp, axis=1)
    sin = jnp.concatenate([sin_ref[...]] * ngrp, axis=1)
    lane = lax.broadcasted_iota(I32, (1, W_DIL), 1)
    first = (lane % HEAD_DIM) < (HEAD_DIM // 2)

    def rot(x):
        up = pltpu.roll(x, W_DIL - HEAD_DIM // 2, 1)
        dn = pltpu.roll(x, HEAD_DIM // 2, 1)
        return x * cos + jnp.where(first, up, dn) * sin

    qkv = (rot(q_ref[...].astype(F32)) * (HEAD_DIM ** -0.5), rot(k_ref[...].astype(F32)), v_ref[...].astype(F32))
    for g, (_, dil) in enumerate(DIL_PATTERN):
        for j, val in enumerate(qkv):
            if dil == 1:
                outs[3 * g + j][0] = val[:, g * LANE:(g + 1) * LANE].astype(BF16)
                continue
            stage_ref[j] = val[:, g * LANE:(g + 1) * LANE]
            for c in range(dil):
                outs[3 * g + j][c] = stage_ref[j, pl.ds(c, tt // dil, stride=dil), :].astype(BF16)


def _rope(proj3, cos_t, sin_t, tt=512):
    b, s, _ = proj3.shape
    tt = min(tt, s)
    c0 = C_DIL // W_DIL
    spec = lambda c: pl.BlockSpec((None, tt, W_DIL), lambda bi, i: (bi, i, c))
    tspec = pl.BlockSpec((tt, LANE), lambda bi, i: (i, 0))
    out_shape, out_specs = [], []
    for _, dil in DIL_PATTERN:
        for _ in range(3):
            out_shape.append(jax.ShapeDtypeStruct((b, dil, s // dil, LANE), BF16))
            out_specs.append(pl.BlockSpec((None, dil, tt // dil, LANE), lambda bi, i: (bi, 0, i, 0)))
    outs = pl.pallas_call(
        _rope_kernel,
        out_shape=tuple(out_shape),
        grid=(b, s // tt),
        in_specs=[spec(c0), spec(c0 + 1), spec(c0 + 2), tspec, tspec],
        out_specs=tuple(out_specs),
        scratch_shapes=[pltpu.VMEM((3, tt, LANE), F32)],
        compiler_params=_cparams(("parallel", "parallel")),
        name="rope",
    )(proj3, proj3, proj3, cos_t, sin_t)
    return [outs[3 * g:3 * g + 3] for g in range(len(DIL_PATTERN))]


DIL_QB = 256


def _dil_kernel(length, half, q_ref, kp_ref, kc_ref, kn_ref, vp_ref, vc_ref, vn_ref, o_ref, lse_ref):
    i = pl.program_id(2)
    qb = q_ref.shape[0]
    hb = kp_ref.shape[0]
    lane = lax.broadcasted_iota(I32, (1, LANE), 1)
    m0 = (lane < HEAD_DIM)
    q = q_ref[...]
    zero = jnp.zeros_like(q)
    qs = jnp.concatenate([jnp.where(m0, q, zero), jnp.where(m0, zero, q)], axis=0)
    kcat = jnp.concatenate([kp_ref[...], kc_ref[...], kn_ref[...]], axis=0)
    vcat = jnp.concatenate([vp_ref[...], vc_ref[...], vn_ref[...]], axis=0)
    s = _dot(qs, kcat, NT)
    qpos = i * qb + lax.broadcasted_iota(I32, (2 * qb, 1), 0) % qb
    kpos = i * qb - hb + lax.broadcasted_iota(I32, (1, qb + 2 * hb), 1)
    valid = (jnp.abs(qpos - kpos) <= half) & (kpos >= 0) & (kpos < length)
    s = jnp.where(valid, s, NEG_BIG)
    m = jnp.max(s, axis=-1, keepdims=True)
    p = jnp.where(valid, jnp.exp(s - m), 0.0)
    den = jnp.sum(p, axis=-1, keepdims=True)
    o = _dot(p, vcat) / den
    lse = m + jnp.log(den)
    o_ref[...] = jnp.where(m0, o[:qb], o[qb:])
    lse_ref[...] = jnp.where(m0, lse[:qb], lse[qb:])


def _dil_group(q, k, v, g):
    win, dil = DIL_PATTERN[g]
    half = win // (2 * dil)
    b, _, length, _ = q.shape
    qb = min(DIL_QB, length)
    hb = min(LANE, qb)
    assert half <= hb
    ratio = qb // hb
    nhb = length // hb
    cur = pl.BlockSpec((None, None, qb, LANE), lambda bi, c, i: (bi, c, i, 0))
    prev = pl.BlockSpec((None, None, hb, LANE), lambda bi, c, i: (bi, c, jnp.maximum(i * ratio - 1, 0), 0))
    nxt = pl.BlockSpec((None, None, hb, LANE), lambda bi, c, i: (bi, c, jnp.minimum((i + 1) * ratio, nhb - 1), 0))
    out = jax.ShapeDtypeStruct((b, dil, length, LANE), F32)
    return pl.pallas_call(
        functools.partial(_dil_kernel, length, half),
        out_shape=(out, out),
        grid=(b, dil, length // qb),
        in_specs=[cur, prev, cur, nxt, prev, cur, nxt],
        out_specs=(cur, cur),
        compiler_params=_cparams(("parallel", "parallel", "parallel")),
        name=f"dilated_attention_g{g}",
    )(q, k, k, k, v, v, v)


GDN_TT = 512
N_CHAIN = 2 * H_GDN
W_CHAIN = N_CHAIN * HEAD_DIM
CHAIN_TILE = 256


def _head_ones(width):
    idx = np.arange(width) // HEAD_DIM
    return jnp.asarray((idx[:, None] == idx[None, :]).astype(np.float32))


def _gdn_prep_kernel(seq_tiles, x_ref, xp_ref, xn_ref, ba_ref, cw_ref, alog_ref, dtb_ref, ones_ref, eb_ref,
                     q_ref, k_ref, v_ref, beta_ref, gc_ref):
    i = pl.program_id(1)
    x = x_ref[...].astype(F32)
    tt = x.shape[0]
    halo = xp_ref.shape[0]
    row = lax.broadcasted_iota(I32, (tt, 1), 0)
    prev = jnp.where(i > 0, xp_ref[...].astype(F32), 0.0)
    nxt = jnp.where(i < seq_tiles - 1, xn_ref[...].astype(F32), 0.0)
    acc = x * cw_ref[CONV_K // 2:CONV_K // 2 + 1, :]
    for d in (1, 2):
        dn = pltpu.roll(x, d, 0)
        for e in range(d):
            dn = jnp.where(row == e, prev[halo - d + e:halo - d + e + 1, :], dn)
        acc = acc + dn * cw_ref[CONV_K // 2 - d:CONV_K // 2 - d + 1, :]
        up = pltpu.roll(x, tt - d, 0)
        for e in range(d):
            up = jnp.where(row == tt - d + e, nxt[e:e + 1, :], up)
        acc = acc + up * cw_ref[CONV_K // 2 + d:CONV_K // 2 + d + 1, :]
    act = acc * _sigmoid(acc)
    q, k, v = act[:, :W_GDN], act[:, W_GDN:2 * W_GDN], act[:, 2 * W_GDN:]
    ones = ones_ref[...]

    def head_sumsq(t):
        return sum(_dot(piece, ones) for piece in _split_bf16(t * t, 2))

    q_ref[...] = q * lax.rsqrt(head_sumsq(q) + NORM_EPS) * (HEAD_DIM ** -0.5)
    k_ref[...] = k * lax.rsqrt(head_sumsq(k) + NORM_EPS)
    v_ref[...] = v
    ba = ba_ref[...]
    beta = _sigmoid(ba)
    z = ba + dtb_ref[...]
    g = -jnp.exp(alog_ref[...]) * (jnp.maximum(z, 0.0) + jnp.log1p(jnp.exp(-jnp.abs(z))))
    lane = lax.broadcasted_iota(I32, (1, LANE), 1)
    bg = jnp.where(lane < N_CHAIN, beta, g)
    eb = eb_ref[...]
    expd = sum(_dot(piece, eb) for piece in _split_bf16(bg, 3))
    beta_ref[...] = expd[:, :W_CHAIN]
    pos = row % GDN_CHUNK
    gc_f = expd[:, W_CHAIN:W_CHAIN + W_GDN]
    gc_b = expd[:, W_CHAIN + W_GDN:]
    sh = 1
    while sh < GDN_CHUNK:
        gc_f = gc_f + jnp.where(pos >= sh, pltpu.roll(gc_f, sh, 0), 0.0)
        gc_b = gc_b + jnp.where(pos < GDN_CHUNK - sh, pltpu.roll(gc_b, tt - sh, 0), 0.0)
        sh *= 2
    gc_ref[...] = jnp.concatenate([gc_f, gc_b], axis=1)


GDN_HALO = 16


def _gdn_prep(proj3, ba3, conv_w, a_log, dt_bias):
    b, s, _ = proj3.shape
    tt = min(GDN_TT, s)
    nt = s // tt
    w3 = 3 * W_GDN
    cw = jnp.zeros((8, w3), F32).at[:CONV_K].set(conv_w)
    alog_row = jnp.zeros((1, LANE), F32).at[0, N_CHAIN:2 * N_CHAIN].set(a_log.reshape(-1))
    dtb_row = jnp.zeros((1, LANE), F32).at[0, N_CHAIN:2 * N_CHAIN].set(dt_bias.reshape(-1))
    eb = np.zeros((LANE, 2 * W_CHAIN), np.float32)
    for c in range(N_CHAIN):
        eb[c, c * HEAD_DIM:(c + 1) * HEAD_DIM] = 1.0
        eb[N_CHAIN + c, W_CHAIN + c * HEAD_DIM:W_CHAIN + (c + 1) * HEAD_DIM] = 1.0
    blk = tt // GDN_HALO
    out_w = lambda w: jax.ShapeDtypeStruct((b, s, w), F32)
    const = lambda a: pl.BlockSpec(a.shape, lambda bi, i: (0,) * a.ndim)
    ones = _head_ones(W_GDN)
    eb = jnp.asarray(eb)
    ospec = lambda w: pl.BlockSpec((None, tt, w), lambda bi, i: (bi, i, 0))
    return pl.pallas_call(
        functools.partial(_gdn_prep_kernel, nt),
        out_shape=(out_w(W_GDN), out_w(W_GDN), out_w(W_GDN), out_w(W_CHAIN), out_w(W_CHAIN)),
        grid=(b, nt),
        in_specs=[pl.BlockSpec((None, tt, w3), lambda bi, i: (bi, i, 0)),
                  pl.BlockSpec((None, GDN_HALO, w3), lambda bi, i: (bi, jnp.maximum(i * blk - 1, 0), 0)),
                  pl.BlockSpec((None, GDN_HALO, w3),
                               lambda bi, i: (bi, jnp.minimum((i + 1) * blk, s // GDN_HALO - 1), 0)),
                  pl.BlockSpec((None, tt, LANE), lambda bi, i: (bi, i, 0)),
                  const(cw), const(alog_row), const(dtb_row), const(ones), const(eb)],
        out_specs=(ospec(W_GDN), ospec(W_GDN), ospec(W_GDN), ospec(W_CHAIN), ospec(W_CHAIN)),
        compiler_params=_cparams(("parallel", "parallel")),
        name="gdn_prep",
    )(proj3, proj3, proj3, ba3, cw, alog_row, dtb_row, ones, eb)


GDN_CS = 4


def _gdn_consts():
    c = GDN_CHUNK
    i = np.arange(c)[:, None]
    lane = np.arange(W_CHAIN)[None, :]
    j = lane % HEAD_DIM
    fwd = lane < W_GDN
    incl = np.where(fwd, j <= i, j >= i)
    strict = np.where(fwd, j < i, j > i)
    eye = (j == i)
    masks = np.stack([incl, strict, eye]).astype(np.float32)
    r = np.arange(CHAIN_TILE)[:, None] // HEAD_DIM
    cidx = np.arange(CHAIN_TILE)[None, :] // HEAD_DIM
    bd = (r == cidx).astype(np.float32)
    return jnp.asarray(masks), jnp.asarray(bd)


def _gdn_kernel(qf_ref, kf_ref, vf_ref, qb_ref, kb_ref, vb_ref, betaf_ref, betab_ref, gcf_ref, gcb_ref,
                masks_ref, bd_ref, of_ref, ob_ref, state_ref):
    @pl.when(pl.program_id(1) == 0)
    def _():
        state_ref[...] = jnp.zeros_like(state_ref)

    c = GDN_CHUNK
    ntile = W_CHAIN // CHAIN_TILE
    incl = masks_ref[0]
    strict = masks_ref[1]
    eye = masks_ref[2]
    bd = bd_ref[...]
    bd16 = bd.astype(BF16)
    reps = CHAIN_TILE // HEAD_DIM

    def bdiag(r_t):
        return jnp.concatenate([r_t.astype(BF16)] * reps, axis=0) * bd16

    def chain_mm(lhs, rhs):
        outs = []
        for t in range(ntile):
            sl = slice(t * CHAIN_TILE, (t + 1) * CHAIN_TILE)
            outs.append(_dot(lhs[:, sl], bdiag(rhs[:, sl])))
        return jnp.concatenate(outs, axis=1)

    for step in range(GDN_CS):
        fs = slice(step * c, (step + 1) * c)
        bs = slice((GDN_CS - 1 - step) * c, (GDN_CS - step) * c)
        cat = lambda a, bb: jnp.concatenate([a, bb], axis=1)
        q = cat(qf_ref[fs, :], qb_ref[bs, :])
        k = cat(kf_ref[fs, :], kb_ref[bs, :])
        v = cat(vf_ref[fs, :], vb_ref[bs, :])
        beta = cat(betaf_ref[fs, :W_GDN], betab_ref[bs, W_GDN:])
        gcol = cat(gcf_ref[fs, :W_GDN], gcb_ref[bs, W_GDN:])
        grow = jnp.sum(gcol * eye, axis=0, keepdims=True)
        glast = cat(gcol[c - 1:c, :W_GDN], gcol[0:1, W_GDN:])
        decay = jnp.where(incl > 0, jnp.exp(jnp.where(incl > 0, gcol - grow, 0.0)), 0.0)
        kb = k * beta
        lhs = jnp.concatenate([kb, q], axis=0)
        sc = []
        for t in range(ntile):
            sl = slice(t * CHAIN_TILE, (t + 1) * CHAIN_TILE)
            sc.append(_dot(lhs[:, sl], bdiag(k[:, sl]), NT))
        sc = jnp.concatenate(sc, axis=1)
        a_mat = sc[:c] * decay * strict
        intra = sc[c:] * decay
        p = eye - a_mat
        x = chain_mm(a_mat, a_mat)
        for it in range(5):
            if it < 4:
                both = chain_mm(jnp.concatenate([p, x], axis=0), x)
                p = p + both[:c]
                x = both[c:]
            else:
                p = p + chain_mm(p, x)
        egc = jnp.exp(gcol)
        w = chain_mm(p, kb * egc)
        u = chain_mm(p, v * beta)
        q_dec = q * egc
        k_dec = k * jnp.exp(glast - gcol)
        g_last = jnp.exp(glast)
        lhs2 = jnp.concatenate([w, q_dec], axis=0)
        ws_qs = jnp.concatenate(
            [_dot(lhs2[:, t * CHAIN_TILE:(t + 1) * CHAIN_TILE], state_ref[t]) for t in range(ntile)], axis=1)
        v_new = u - ws_qs[:c]
        o = ws_qs[c:] + chain_mm(intra, v_new)
        for t in range(ntile):
            sl = slice(t * CHAIN_TILE, (t + 1) * CHAIN_TILE)
            state_ref[t] = state_ref[t] * g_last[:, sl] + _dot(k_dec[:, sl], v_new[:, sl], TN) * bd
        of_ref[fs, :] = o[:, :W_GDN]
        ob_ref[bs, :] = o[:, W_GDN:]


def _gdn_main(qn, kn, vv, beta, gc):
    b, s, _ = qn.shape
    ts = GDN_CS * GDN_CHUNK
    nb = s // ts
    masks, bd = _gdn_consts()
    fwd = lambda w: pl.BlockSpec((None, ts, w), lambda bi, i: (bi, i, 0))
    bwd = lambda w: pl.BlockSpec((None, ts, w), lambda bi, i: (bi, nb - 1 - i, 0))
    const = lambda a: pl.BlockSpec(a.shape, lambda bi, i: (0,) * a.ndim)
    out = jax.ShapeDtypeStruct((b, s, W_GDN), F32)
    return pl.pallas_call(
        _gdn_kernel,
        out_shape=(out, out),
        grid=(b, nb),
        in_specs=[fwd(W_GDN), fwd(W_GDN), fwd(W_GDN), bwd(W_GDN), bwd(W_GDN), bwd(W_GDN),
                  fwd(W_CHAIN), bwd(W_CHAIN), fwd(W_CHAIN), bwd(W_CHAIN), const(masks), const(bd)],
        out_specs=(fwd(W_GDN), bwd(W_GDN)),
        scratch_shapes=[pltpu.VMEM((W_CHAIN // CHAIN_TILE, CHAIN_TILE, CHAIN_TILE), F32)],
        compiler_params=_cparams(("parallel", "arbitrary")),
        name="gdn_scan",
    )(qn, kn, vv, qn, kn, vv, beta, beta, gc, gc, masks, bd)


TILE_ROWS = D_MODEL // LANE


def _store_token_tiles(ref, val):
    t = val.shape[0]
    for j in range(TILE_ROWS):
        ref[pl.ds(j, t, stride=TILE_ROWS), :] = val[:, j * LANE:(j + 1) * LANE]


def _load_token_tiles(ref, t):
    return jnp.concatenate([ref[pl.ds(j, t, stride=TILE_ROWS), :] for j in range(TILE_ROWS)], axis=1)


def _layer_norm(y, g, b):
    mu = jnp.mean(y, axis=-1, keepdims=True)
    d = y - mu
    var = jnp.mean(d * d, axis=-1, keepdims=True)
    return d * lax.rsqrt(var + LN_EPS) * g + b


def _split_bf16(a, terms):
    pieces, rest = [], a
    for _ in range(terms):
        piece = rest.astype(BF16)
        pieces.append(piece)
        rest = rest - piece.astype(F32)
    return pieces


def _merge_kernel(x_ref, gates_ref, na_ref, o0, s0, o1, s1, o2, s2, of_ref, ob_ref, z_ref,
                  wna_ref, wdil_ref, wgdn_ref, wout_ref, normw_ref, ones_ref, g_ref, b_ref, wrh_ref, wrl_ref,
                  x1_ref, xa_ref, aff_ref, cls_ref):
    vals = []
    for slot, ref in enumerate((o0, s0, o1, s1, o2, s2)):
        if len(ref.shape) == 2:
            vals.append(ref[...])
        else:
            dil, rows, _ = ref.shape
            for c in range(dil):
                cls_ref[slot, pl.ds(c, rows, stride=dil), :] = ref[c]
            vals.append(cls_ref[slot])
    lses = vals[1::2]
    mm = jnp.maximum(jnp.maximum(lses[0], lses[1]), lses[2])
    wts = [jnp.exp(l - mm) for l in lses]
    o_dil = (wts[0] * vals[0] + wts[1] * vals[2] + wts[2] * vals[4]) / (wts[0] + wts[1] + wts[2])
    o = of_ref[...] + ob_ref[...]
    ones = ones_ref[...]
    ms = sum(_dot(piece, ones) for piece in _split_bf16(o * o, 2)) * (1.0 / HEAD_DIM)
    z = z_ref[...].astype(F32)
    o_gdn = o * lax.rsqrt(ms + NORM_EPS) * normw_ref[...] * (z * _sigmoid(z))
    br_na = _dot(na_ref[...], wna_ref[...])
    br_dil = _dot(o_dil, wdil_ref[...])
    br_gdn = _dot(o_gdn, wgdn_ref[...])
    gate = _sigmoid(gates_ref[...].astype(F32))
    merged = (gate[:, :D_MODEL] * br_na + gate[:, D_MODEL:2 * D_MODEL] * br_dil
              + gate[:, 2 * D_MODEL:] * br_gdn)
    mix = _dot(merged, wout_ref[...])
    x1 = _layer_norm(DN_ALPHA * x_ref[...] + mix, g_ref[...], b_ref[...])
    _store_token_tiles(x1_ref, x1)
    _store_token_tiles(xa_ref, DN_ALPHA * x1)
    xh, xl = _split_bf16(x1, 2)
    logits = _dot(xh, wrh_ref[...]) + _dot(xl, wrh_ref[...]) + _dot(xh, wrl_ref[...])
    e = jnp.exp(logits - jnp.max(logits, axis=-1, keepdims=True))
    aff_ref[...] = e / jnp.sum(e, axis=-1, keepdims=True)


def _merge(x, proj, o_na, dil, o_f, o_b, b, s, w_br_na, w_br_dil, w_br_gdn, w_out, norm_w, ln_g, ln_b, w_router,
           tm=256):
    n = b * s
    tm = min(tm, s)
    nt = s // tm
    tok = lambda w, c=0: pl.BlockSpec((tm, w), lambda bi, i: (bi * nt + i, c))
    const = lambda a: pl.BlockSpec(a.shape, lambda bi, i: (0,) * a.ndim)
    normw = jnp.tile(norm_w, H_GDN).reshape(1, W_GDN)
    ones = _head_ones(W_GDN)
    wr_hi = w_router.astype(BF16)
    wr_lo = (w_router - wr_hi.astype(F32)).astype(BF16)
    consts = [w_br_na.astype(BF16), w_br_dil.astype(BF16), w_br_gdn.astype(BF16), w_out.astype(BF16),
              normw, ones, ln_g.reshape(1, -1), ln_b.reshape(1, -1), wr_hi, wr_lo]
    dil_flat, dil_specs = [], []
    for (_, d), grp in zip(DIL_PATTERN, dil):
        for t in grp:
            dil_flat.append(t)
            if d == 1:
                dil_specs.append(pl.BlockSpec((None, None, tm, LANE), lambda bi, i: (bi, 0, i, 0)))
            else:
                dil_specs.append(pl.BlockSpec((None, d, tm // d, LANE), lambda bi, i: (bi, 0, i, 0)))
    big = jax.ShapeDtypeStruct((n * TILE_ROWS, LANE), F32)
    tile_spec = pl.BlockSpec((tm * TILE_ROWS, LANE), lambda bi, i: (bi * nt + i, 0))
    return pl.pallas_call(
        _merge_kernel,
        out_shape=(big, big, jax.ShapeDtypeStruct((n, N_EXPERTS), F32)),
        grid=(b, nt),
        in_specs=[tok(D_MODEL), tok(N_BRANCH * D_MODEL, C_GATE // (N_BRANCH * D_MODEL)), tok(W_NA)]
                 + dil_specs + [tok(W_GDN), tok(W_GDN), tok(W_GDN, C_Z // W_GDN)]
                 + [const(a) for a in consts],
        out_specs=(tile_spec, tile_spec, tok(N_EXPERTS)),
        scratch_shapes=[pltpu.VMEM((2 * len(DIL_PATTERN), tm, LANE), F32)],
        compiler_params=_cparams(("parallel", "parallel")),
        name="merge_ln1_router",
    )(x, proj, o_na, *dil_flat, o_f, o_b, proj, *consts)


def _thresh_kernel(cap, aff_ref, tau_ref, cgt_ref):
    bits = pltpu.bitcast(aff_ref[...], I32)
    ne = bits.shape[0]

    def count_ge(t):
        return jnp.sum((bits >= t).astype(I32), axis=1, keepdims=True)

    def body(_, carry):
        lo, hi = carry
        mid = lo + jnp.right_shift(hi - lo, 1)
        ok = count_ge(mid) >= cap
        return jnp.where(ok, mid, lo), jnp.where(ok, hi, mid)

    lo0 = jnp.zeros((ne, 1), I32)
    hi0 = jnp.full((ne, 1), 0x7F800000, I32)
    lo, _ = lax.fori_loop(0, 31, body, (lo0, hi0))
    tau_ref[...] = jnp.broadcast_to(lo, tau_ref.shape)
    cgt_ref[...] = jnp.broadcast_to(jnp.sum((bits > lo).astype(I32), axis=1, keepdims=True), cgt_ref.shape)


def _thresholds(aff_t, cap):
    ne, n = aff_t.shape
    out = jax.ShapeDtypeStruct((ne, LANE), I32)
    return pl.pallas_call(
        functools.partial(_thresh_kernel, cap),
        out_shape=(out, out),
        in_specs=[pl.BlockSpec((ne, n), lambda: (0, 0))],
        out_specs=(pl.BlockSpec((ne, LANE), lambda: (0, 0)),) * 2,
        compiler_params=pltpu.CompilerParams(vmem_limit_bytes=VMEM_LIMIT),
        name="expert_thresholds",
    )(aff_t)


CMP_TC = 256
CMP_WIN = CMP_TC // LANE + 1


def _compact_kernel(aff_ref, tau_ref, take_ref, tri_ref, idx_ref, gate_ref, carry_ref):
    step = pl.program_id(0)

    @pl.when(step == 0)
    def _():
        idx_ref[...] = jnp.zeros_like(idx_ref)
        gate_ref[...] = jnp.zeros_like(gate_ref)
        carry_ref[...] = jnp.zeros_like(carry_ref)

    aff = aff_ref[...]
    tc = aff.shape[0]
    bits = pltpu.bitcast(aff, I32)
    tau = tau_ref[...]
    gt = (bits > tau).astype(F32)
    eq = (bits == tau).astype(F32)
    tri = tri_ref[...]
    carry = carry_ref[...]
    tie_incl = _dot(tri, eq) + carry[1:2, :]
    take = eq * ((tie_incl - eq) < take_ref[...].astype(F32)).astype(F32)
    sel = gt + take
    pos_incl = _dot(tri, sel) + carry[0:1, :]
    pos = pos_incl - sel
    carry_ref[0:1, :] = pos_incl[tc - 1:tc, :]
    carry_ref[1:2, :] = tie_incl[tc - 1:tc, :]
    tok = step * tc + lax.broadcasted_iota(I32, (tc, 1), 0)
    lane = lax.broadcasted_iota(I32, (1, LANE), 1)
    for e in range(N_EXPERTS):
        start = carry[0, e].astype(I32)
        r0 = start // LANE
        rel = pos[:, e:e + 1].astype(I32) - r0 * LANE
        sel_e = sel[:, e:e + 1] > 0
        aff_e = aff[:, e:e + 1]
        for j in range(CMP_WIN):
            hit = (rel == lane + j * LANE) & sel_e
            idx_ref[e, r0 + j] += jnp.sum(jnp.where(hit, tok, 0), axis=0, keepdims=True)
            gate_ref[e, r0 + j] += jnp.sum(jnp.where(hit, aff_e, 0.0), axis=0, keepdims=True)


def _compact(aff, tau_row, take_row, cap):
    n, ne = aff.shape
    tc = min(CMP_TC, n)
    rows = cap // LANE + CMP_WIN
    t = np.arange(tc)
    tri = jnp.asarray((t[None, :] <= t[:, None]).astype(np.float32))
    idx, gate = pl.pallas_call(
        _compact_kernel,
        out_shape=(jax.ShapeDtypeStruct((ne, rows, 1, LANE), I32),
                   jax.ShapeDtypeStruct((ne, rows, 1, LANE), F32)),
        grid=(n // tc,),
        in_specs=[pl.BlockSpec((tc, ne), lambda i: (i, 0)),
                  pl.BlockSpec((1, ne), lambda i: (0, 0)),
                  pl.BlockSpec((1, ne), lambda i: (0, 0)),
                  pl.BlockSpec((tc, tc), lambda i: (0, 0))],
        out_specs=(pl.BlockSpec((ne, rows, 1, LANE), lambda i: (0, 0, 0, 0)),) * 2,
        scratch_shapes=[pltpu.VMEM((2, ne), F32)],
        compiler_params=_cparams(("arbitrary",)),
        name="expert_compaction",
    )(aff, tau_row, take_row, tri)
    idx = idx[:, :cap // LANE].reshape(ne, cap)
    gate = gate[:, :cap // LANE].reshape(ne, cap)
    return idx, gate


MOE_TM = 256


def _moe_kernel(idx_ref, idxn_ref, gate_ref, x_hbm, acc_in, wg_ref, wu_ref, wd_ref, acc_hbm, xbuf, ybuf, sems,
                order_sem):
    del acc_in
    nk = pl.num_programs(1)
    step = pl.program_id(0) * nk + pl.program_id(1)
    last = pl.num_programs(0) * nk - 1
    slot = step % 2
    tm = gate_ref.shape[0]
    sem_x, sem_y, sem_s = 0, 2, 3

    rows = tm * TILE_ROWS

    def token(hbm, ref, r):
        return hbm.at[pl.ds(pl.multiple_of(ref[0, r] * TILE_ROWS, TILE_ROWS), TILE_ROWS), :]

    def tile(r):
        return pl.ds(r * TILE_ROWS, TILE_ROWS)

    def x_row(ref, r, sl):
        return pltpu.make_async_copy(token(x_hbm, ref, r), xbuf.at[sl, tile(r), :], sems.at[sem_x + sl])

    def x_tile_wait(sl):
        pltpu.make_async_copy(x_hbm.at[pl.ds(0, rows), :], xbuf.at[sl], sems.at[sem_x + sl]).wait()

    def scatter_wait():
        pltpu.make_async_copy(ybuf, acc_hbm.at[pl.ds(0, rows), :], sems.at[sem_s]).wait()

    @pl.when(step == 0)
    def _():
        for r in range(tm):
            x_row(idx_ref, r, 0).start()

    x_tile_wait(slot)
    for r in range(tm):
        x_row(idxn_ref, r, 1 - slot).start()
    xb = _load_token_tiles(xbuf.at[slot], tm).astype(BF16)
    hg = jnp.dot(xb, wg_ref[...], preferred_element_type=F32)
    hu = jnp.dot(xb, wu_ref[...], preferred_element_type=F32)
    hid = (hg * _sigmoid(hg) * hu).astype(BF16)

    @pl.when(step > 0)
    def _():
        scatter_wait()

    for r in range(tm):
        pltpu.make_async_copy(token(acc_hbm, idx_ref, r), ybuf.at[tile(r), :], sems.at[sem_y]).start()
    pl.semaphore_signal(order_sem, 1)
    pl.semaphore_wait(order_sem, 1)
    ye = jnp.dot(hid, wd_ref[...], preferred_element_type=F32) * gate_ref[...]
    pltpu.make_async_copy(acc_hbm.at[pl.ds(0, rows), :], ybuf, sems.at[sem_y]).wait()
    _store_token_tiles(ybuf, _load_token_tiles(ybuf, tm) + ye)
    for r in range(tm):
        pltpu.make_async_copy(ybuf.at[tile(r), :], token(acc_hbm, idx_ref, r), sems.at[sem_s]).start()

    @pl.when(step == last)
    def _():
        scatter_wait()
        x_tile_wait(1 - slot)


def _moe(x1, acc, idx, gate, wg, wu, wd):
    d = wg.shape[1]
    ne, cap = idx.shape
    tm = min(MOE_TM, cap)
    nk = cap // tm
    f = wg.shape[2]
    idx3 = idx.reshape(ne, nk, 1, tm)
    gate4 = gate.reshape(ne, nk, tm, 1)

    def nxt(e, k):
        s = jnp.minimum(e * nk + k + 1, ne * nk - 1)
        return (s // nk, s % nk, 0, 0)

    return pl.pallas_call(
        _moe_kernel,
        out_shape=jax.ShapeDtypeStruct(acc.shape, F32),
        grid=(ne, nk),
        in_specs=[pl.BlockSpec((None, None, 1, tm), lambda e, k: (e, k, 0, 0), memory_space=pltpu.SMEM),
                  pl.BlockSpec((None, None, 1, tm), nxt, memory_space=pltpu.SMEM),
                  pl.BlockSpec((None, None, tm, 1), lambda e, k: (e, k, 0, 0)),
                  pl.BlockSpec(memory_space=pl.ANY),
                  pl.BlockSpec(memory_space=pl.ANY),
                  pl.BlockSpec((None, d, f), lambda e, k: (e, 0, 0)),
                  pl.BlockSpec((None, d, f), lambda e, k: (e, 0, 0)),
                  pl.BlockSpec((None, f, d), lambda e, k: (e, 0, 0))],
        out_specs=pl.BlockSpec(memory_space=pl.ANY),
        scratch_shapes=[pltpu.VMEM((2, tm * TILE_ROWS, LANE), F32), pltpu.VMEM((tm * TILE_ROWS, LANE), F32),
                        pltpu.SemaphoreType.DMA((4,)), pltpu.SemaphoreType.REGULAR],
        input_output_aliases={4: 0},
        compiler_params=pltpu.CompilerParams(dimension_semantics=("arbitrary", "arbitrary"),
                                             vmem_limit_bytes=VMEM_LIMIT, has_side_effects=True),
        name="expert_ffn",
    )(idx3, idx3, gate4, x1, acc, wg, wu, wd)


def _ln_kernel(x_ref, g_ref, b_ref, o_ref):
    o_ref[...] = _layer_norm(_load_token_tiles(x_ref, o_ref.shape[0]), g_ref[...], b_ref[...])


def _ln(x, g, b, tm=512):
    n, d = x.shape[0] // TILE_ROWS, D_MODEL
    tm = min(tm, n)
    return pl.pallas_call(
        _ln_kernel,
        out_shape=jax.ShapeDtypeStruct((n, d), F32),
        grid=(n // tm,),
        in_specs=[pl.BlockSpec((tm * TILE_ROWS, LANE), lambda i: (i, 0)), pl.BlockSpec((1, d), lambda i: (0, 0)),
                  pl.BlockSpec((1, d), lambda i: (0, 0))],
        out_specs=pl.BlockSpec((tm, d), lambda i: (i, 0)),
        compiler_params=_cparams(("parallel",)),
        name="layer_norm2",
    )(x, g.reshape(1, d), b.reshape(1, d))


def _token_mixer_parts(x, b, s, w_in_p, bias_tab, conv_w, a_log, dt_bias, rope_tabs):
    n = b * s
    proj, ba = _inproj(x, *w_in_p)
    proj3 = proj.reshape(b, s, C_BA)
    o_na = _na_attention(proj3, bias_tab).reshape(n, W_NA)
    dil = [_dil_group(*qkv, g) for g, qkv in enumerate(_rope(proj3, *rope_tabs))]
    qn, kn, vv, beta, gc = _gdn_prep(proj3, ba.reshape(b, s, LANE), conv_w, a_log, dt_bias)
    o_f, o_b = _gdn_main(qn, kn, vv, beta, gc)
    return proj, o_na, dil, o_f.reshape(n, W_GDN), o_b.reshape(n, W_GDN)


def _expert_choice(x1, xa, aff, wg, wu, wd):
    n = aff.shape[0]
    cap = EC_FACTOR * n // N_EXPERTS
    tau, cgt = _thresholds(aff.T, cap)
    tau_row = tau[:, 0].reshape(1, N_EXPERTS)
    take_row = (cap - cgt[:, 0]).reshape(1, N_EXPERTS)
    idx, gate = _compact(aff, tau_row, take_row, cap)
    return _moe(x1, xa, idx, gate, wg, wu, wd)


def _trunk(x3, params, shared):
    b, s, d = x3.shape
    n = b * s
    x = x3.reshape(n, d)
    rope_tabs = _rope_tables(s)
    for l in range(DEPTH):
        p = {k: v[l] for k, v in params.items()}
        sh = shared[l]
        proj, o_na, dil, o_f, o_b = _token_mixer_parts(x, b, s, sh["w_in"], sh["bias_tab"], p["conv_w"],
                                                       p["a_log"], p["dt_bias"], rope_tabs)
        x1, xa, aff = _merge(x, proj, o_na, dil, o_f, o_b, b, s, p["w_br_na"], p["w_br_dil"], p["w_br_gdn"],
                             p["w_out"], p["gdn_norm_w"], p["ln1_g"], p["ln1_b"], p["w_router"])
        acc = _expert_choice(x1, xa, aff, sh["w_gate"], sh["w_up"], sh["w_down"])
        x = _ln(acc, p["ln2_g"], p["ln2_b"])
    return x.reshape(b, s, d)


def kernel(x_prompt, x_sample, w_in, na_rpb, conv_w, a_log, dt_bias, gdn_norm_w, w_br_na, w_br_dil, w_br_gdn,
           w_out, ln1_g, ln1_b, w_router, w_up, w_gate, w_down, ln2_g, ln2_b):
    params = dict(conv_w=conv_w, a_log=a_log, dt_bias=dt_bias, gdn_norm_w=gdn_norm_w, w_br_na=w_br_na,
                  w_br_dil=w_br_dil, w_br_gdn=w_br_gdn, w_out=w_out, ln1_g=ln1_g, ln1_b=ln1_b,
                  w_router=w_router, ln2_g=ln2_g, ln2_b=ln2_b)
    shared = [dict(w_in=_reorder_w_in(w_in[l]), bias_tab=_na_bias_table(na_rpb[l]),
                   w_gate=w_gate[l].astype(BF16), w_up=w_up[l].astype(BF16), w_down=w_down[l].astype(BF16))
              for l in range(DEPTH)]
    return _trunk(x_prompt, params, shared), _trunk(x_sample, params, shared)
```

```python
import functools
import math

import numpy as np
import jax
import jax.numpy as jnp
from jax import lax
from jax.experimental import pallas as pl
from jax.experimental.pallas import tpu as pltpu

F32 = jnp.float32
BF16 = jnp.bfloat16
I32 = jnp.int32

D_MODEL = 1024
DEPTH = 2
HEAD_DIM = 64
H_NA = 4
H_DIL = 6
H_GDN = 6
W_NA = H_NA * HEAD_DIM
W_DIL = H_DIL * HEAD_DIM
W_GDN = H_GDN * HEAD_DIM
GRID_W = 64
NA_KH = 8
NA_KW = 16
DIL_PATTERN = ((128, 1), (512, 4), (2048, 16))
DIL_HPG = H_DIL // len(DIL_PATTERN)
W_DIL_OUT = DIL_HPG * HEAD_DIM
ROPE_THETA = 10000.0
GDN_CHUNK = 64
CONV_K = 5
N_EXPERTS = 16
D_EXPERT = 2048
EC_FACTOR = 2
N_BRANCH = 3
DN_ALPHA = (2 * DEPTH) ** 0.25
LN_EPS = 1e-5
NORM_EPS = 1e-6
NEG_BIG = -1e30

C_GDN = 0
C_DIL = C_GDN + 3 * W_GDN
C_NA = C_DIL + 3 * W_DIL
C_GATE = C_NA + 3 * W_NA
C_Z = C_GATE + N_BRANCH * D_MODEL
C_BA = C_Z + W_GDN
LANE = 128
VMEM_LIMIT = 56 * 1024 * 1024


def _cparams(sem):
    return pltpu.CompilerParams(dimension_semantics=sem, vmem_limit_bytes=VMEM_LIMIT)


def _dot(a, b, dims=None, hi=False):
    if dims is None:
        dims = (((a.ndim - 1,), (0,)), ((), ()))
    if hi:
        return lax.dot_general(a.astype(F32), b.astype(F32), dims, precision=lax.Precision.HIGHEST,
                               preferred_element_type=F32)
    return lax.dot_general(a.astype(BF16), b.astype(BF16), dims, preferred_element_type=F32)


NT = (((1,), (1,)), ((), ()))
TN = (((0,), (0,)), ((), ()))


def _sigmoid(x):
    return 1.0 / (1.0 + jnp.exp(-x))


def _inproj_kernel(x_ref, w_ref, wba_ref, o_ref, ba_ref, xb_ref):
    @pl.when(pl.program_id(1) == 0)
    def _():
        xb_ref[...] = x_ref[...].astype(BF16)
        ba_ref[...] = jnp.dot(xb_ref[...], wba_ref[...], preferred_element_type=F32)

    o_ref[...] = jnp.dot(xb_ref[...], w_ref[...], preferred_element_type=F32).astype(BF16)


def _inproj(x, w_main, w_ba, tm=1024, tn=C_BA // 3):
    n, k = x.shape
    tm = min(tm, n)
    return pl.pallas_call(
        _inproj_kernel,
        out_shape=(jax.ShapeDtypeStruct((n, C_BA), BF16), jax.ShapeDtypeStruct((n, LANE), F32)),
        grid=(n // tm, C_BA // tn),
        in_specs=[pl.BlockSpec((tm, k), lambda i, j: (i, 0)),
                  pl.BlockSpec((k, tn), lambda i, j: (0, j)),
                  pl.BlockSpec((k, LANE), lambda i, j: (0, 0))],
        out_specs=(pl.BlockSpec((tm, tn), lambda i, j: (i, j)), pl.BlockSpec((tm, LANE), lambda i, j: (i, 0))),
        scratch_shapes=[pltpu.VMEM((tm, k), BF16)],
        compiler_params=_cparams(("parallel", "arbitrary")),
        name="inproj",
    )(x, w_main, w_ba)


def _reorder_w_in(w):
    o_qd = 3 * W_NA
    o_qc = o_qd + 3 * W_DIL
    o_z = o_qc + 3 * W_GDN
    o_b = o_z + W_GDN
    o_gate = o_b + 4 * H_GDN
    main = jnp.concatenate([w[:, o_qc:o_z], w[:, o_qd:o_qc], w[:, 0:o_qd], w[:, o_gate:], w[:, o_z:o_b]], axis=1)
    ba = jnp.concatenate([w[:, o_b:o_gate], jnp.zeros((w.shape[0], LANE - 4 * H_GDN), w.dtype)], axis=1)
    return main.astype(BF16), ba.astype(BF16)


NA_RQ = 8
NA_WIN_ROWS = 16
NA_KB = 256


def _toeplitz_kernel(rpb_ref, onehot_ref, o_ref):
    o_ref[...] = _dot(rpb_ref[...], onehot_ref[...], hi=True)


def _na_bias_table(rpb):
    h, ndr, ndc = rpb.shape
    qc = np.arange(GRID_W)[:, None]
    kc = np.arange(GRID_W)[None, :]
    dc = np.clip(kc - qc, -(NA_KW - 1), NA_KW - 1) + (NA_KW - 1)
    onehot = (np.arange(LANE)[:, None, None] == dc[None]).astype(np.float32)
    onehot = jnp.asarray(onehot.reshape(LANE, GRID_W * GRID_W))
    rows = -(-h * ndr // 8) * 8
    rpb2 = jnp.zeros((rows, LANE), F32).at[:h * ndr, :ndc].set(rpb.reshape(h * ndr, ndc))
    toe = pl.pallas_call(
        _toeplitz_kernel,
        out_shape=jax.ShapeDtypeStruct((rows, GRID_W * GRID_W), F32),
        name="na_bias_toeplitz",
    )(rpb2, onehot)
    toe = toe[:h * ndr].reshape(h, ndr, GRID_W, GRID_W)
    ws = np.clip(qc - NA_KW // 2, 0, GRID_W - NA_KW)
    ok = jnp.asarray((kc >= ws) & (kc < ws + NA_KW))[None, :, None, :]
    tabs = []
    for off in range(NA_KH):
        lo = NA_KH - 1 - off
        t = jnp.transpose(toe[:, lo:lo + NA_KH], (0, 2, 1, 3))
        tabs.append(jnp.where(ok, t, NEG_BIG).reshape(h * GRID_W, NA_KH * GRID_W))
    return jnp.stack(tabs)


def _na_kernel(rows, q_ref, k0, k1, k2, k3, v0, v1, v2, v3, bias_ref, o_ref, kc_ref, vc_ref):
    rq = pl.program_id(1)
    for j, (kr, vr) in enumerate(((k0, v0), (k1, v1), (k2, v2), (k3, v3))):
        kc_ref[j * NA_KB:(j + 1) * NA_KB, :] = kr[...].astype(BF16)
        vc_ref[j * NA_KB:(j + 1) * NA_KB, :] = vr[...].astype(BF16)
    ws = jnp.clip(rq * NA_RQ - NA_KH // 2, 0, rows - NA_WIN_ROWS)
    lane = lax.broadcasted_iota(I32, (1, W_NA), 1)
    masks = [(lane // HEAD_DIM == h).astype(F32) for h in range(H_NA)]
    for j in range(NA_RQ):
        r = rq * NA_RQ + j
        r0 = jnp.clip(r - NA_KH // 2, 0, rows - NA_KH)
        koff = pl.multiple_of((r0 - ws) * GRID_W, GRID_W)
        off = r - r0
        qj = q_ref[j * GRID_W:(j + 1) * GRID_W, :] * (HEAD_DIM ** -0.5)
        qs = jnp.concatenate([qj * m for m in masks], axis=0).astype(BF16)
        kw = kc_ref[pl.ds(koff, NA_KH * GRID_W), :]
        vw = vc_ref[pl.ds(koff, NA_KH * GRID_W), :]
        s = _dot(qs, kw, NT) + bias_ref[off]
        m = jnp.max(s, axis=-1, keepdims=True)
        p = jnp.exp(s - m)
        den = jnp.sum(p, axis=-1, keepdims=True)
        o = _dot(p, vw) / den
        acc = o[0:GRID_W] * masks[0]
        for h in range(1, H_NA):
            acc = acc + o[h * GRID_W:(h + 1) * GRID_W] * masks[h]
        o_ref[j * GRID_W:(j + 1) * GRID_W, :] = acc


def _na_attention(proj3, bias_tab):
    b, s, _ = proj3.shape
    rows = s // GRID_W
    nrq = rows // NA_RQ
    tq = NA_RQ * GRID_W
    cq, ck, cv = C_NA // W_NA, C_NA // W_NA + 1, C_NA // W_NA + 2
    nkb = s // NA_KB

    def kmap(j, col):
        def f(bi, rq):
            kb0 = jnp.clip(2 * rq - 1, 0, nkb - NA_WIN_ROWS * GRID_W // NA_KB)
            return (bi, kb0 + j, col)
        return f

    in_specs = [pl.BlockSpec((None, tq, W_NA), lambda bi, rq: (bi, rq, cq))]
    in_specs += [pl.BlockSpec((None, NA_KB, W_NA), kmap(j, ck)) for j in range(4)]
    in_specs += [pl.BlockSpec((None, NA_KB, W_NA), kmap(j, cv)) for j in range(4)]
    in_specs += [pl.BlockSpec(bias_tab.shape, lambda bi, rq: (0, 0, 0))]
    return pl.pallas_call(
        functools.partial(_na_kernel, rows),
        out_shape=jax.ShapeDtypeStruct((b, s, W_NA), F32),
        grid=(b, nrq),
        in_specs=in_specs,
        out_specs=pl.BlockSpec((None, tq, W_NA), lambda bi, rq: (bi, rq, 0)),
        scratch_shapes=[pltpu.VMEM((NA_WIN_ROWS * GRID_W, W_NA), BF16),
                        pltpu.VMEM((NA_WIN_ROWS * GRID_W, W_NA), BF16)],
        compiler_params=_cparams(("parallel", "parallel")),
        name="na_attention",
    )(proj3, *([proj3] * 8), bias_tab)


def _rope_tables(s):
    half = HEAD_DIM // 2
    inv = ROPE_THETA ** (-jnp.arange(half, dtype=F32) / half)
    ang = jnp.arange(s, dtype=F32)[:, None] * inv[None, :]
    cos, sin = jnp.cos(ang), jnp.sin(ang)
    cos_t = jnp.concatenate([cos, cos, cos, cos], axis=1)
    sin_t = jnp.concatenate([-sin, sin, -sin, sin], axis=1)
    return cos_t, sin_t


def _rope_kernel(q_ref, k_ref, v_ref, cos_ref, sin_ref, *refs):
    ngrp = len(DIL_PATTERN)
    outs, stage_ref = refs[:3 * ngrp], refs[3 * ngrp]
    tt = q_ref.shape[0]
    cos = jnp.concatenate([cos_ref[...]] * ngrp, axis=1)
    sin = jnp.concatenate([sin_ref[...]] * ngrp, axis=1)
    lane = lax.broadcasted_iota(I32, (1, W_DIL), 1)
    first = (lane % HEAD_DIM) < (HEAD_DIM // 2)

    def rot(x):
        up = pltpu.roll(x, W_DIL - HEAD_DIM // 2, 1)
        dn = pltpu.roll(x, HEAD_DIM // 2, 1)
        return x * cos + jnp.where(first, up, dn) * sin

    qkv = (rot(q_ref[...].astype(F32)) * (HEAD_DIM ** -0.5), rot(k_ref[...].astype(F32)), v_ref[...].astype(F32))
    for g, (_, dil) in enumerate(DIL_PATTERN):
        for j, val in enumerate(qkv):
            if dil == 1:
                outs[3 * g + j][0] = val[:, g * LANE:(g + 1) * LANE].astype(BF16)
                continue
            stage_ref[j] = val[:, g * LANE:(g + 1) * LANE]
            for c in range(dil):
                outs[3 * g + j][c] = stage_ref[j, pl.ds(c, tt // dil, stride=dil), :].astype(BF16)


def _rope(proj3, cos_t, sin_t, tt=512):
    b, s, _ = proj3.shape
    tt = min(tt, s)
    c0 = C_DIL // W_DIL
    spec = lambda c: pl.BlockSpec((None, tt, W_DIL), lambda bi, i: (bi, i, c))
    tspec = pl.BlockSpec((tt, LANE), lambda bi, i: (i, 0))
    out_shape, out_specs = [], []
    for _, dil in DIL_PATTERN:
        for _ in range(3):
            out_shape.append(jax.ShapeDtypeStruct((b, dil, s // dil, LANE), BF16))
            out_specs.append(pl.BlockSpec((None, dil, tt // dil, LANE), lambda bi, i: (bi, 0, i, 0)))
    outs = pl.pallas_call(
        _rope_kernel,
        out_shape=tuple(out_shape),
        grid=(b, s // tt),
        in_specs=[spec(c0), spec(c0 + 1), spec(c0 + 2), tspec, tspec],
        out_specs=tuple(out_specs),
        scratch_shapes=[pltpu.VMEM((3, tt, LANE), F32)],
        compiler_params=_cparams(("parallel", "parallel")),
        name="rope",
    )(proj3, proj3, proj3, cos_t, sin_t)
    return [outs[3 * g:3 * g + 3] for g in range(len(DIL_PATTERN))]


DIL_QB = 256


def _dil_kernel(length, half, q_ref, kp_ref, kc_ref, kn_ref, vp_ref, vc_ref, vn_ref, o_ref, lse_ref):
    i = pl.program_id(2)
    qb = q_ref.shape[0]
    hb = kp_ref.shape[0]
    lane = lax.broadcasted_iota(I32, (1, LANE), 1)
    m0 = (lane < HEAD_DIM)
    q = q_ref[...]
    zero = jnp.zeros_like(q)
    qs = jnp.concatenate([jnp.where(m0, q, zero), jnp.where(m0, zero, q)], axis=0)
    kcat = jnp.concatenate([kp_ref[...], kc_ref[...], kn_ref[...]], axis=0)
    vcat = jnp.concatenate([vp_ref[...], vc_ref[...], vn_ref[...]], axis=0)
    s = _dot(qs, kcat, NT)
    qpos = i * qb + lax.broadcasted_iota(I32, (2 * qb, 1), 0) % qb
    kpos = i * qb - hb + lax.broadcasted_iota(I32, (1, qb + 2 * hb), 1)
    valid = (jnp.abs(qpos - kpos) <= half) & (kpos >= 0) & (kpos < length)
    s = jnp.where(valid, s, NEG_BIG)
    m = jnp.max(s, axis=-1, keepdims=True)
    p = jnp.where(valid, jnp.exp(s - m), 0.0)
    den = jnp.sum(p, axis=-1, keepdims=True)
    o = _dot(p, vcat) / den
    lse = m + jnp.log(den)
    o_ref[...] = jnp.where(m0, o[:qb], o[qb:])
    lse_ref[...] = jnp.where(m0, lse[:qb], lse[qb:])


def _dil_group(q, k, v, g):
    win, dil = DIL_PATTERN[g]
    half = win // (2 * dil)
    b, _, length, _ = q.shape
    qb = min(DIL_QB, length)
    hb = min(LANE, qb)
    assert half <= hb
    ratio = qb // hb
    nhb = length // hb
    cur = pl.BlockSpec((None, None, qb, LANE), lambda bi, c, i: (bi, c, i, 0))
    prev = pl.BlockSpec((None, None, hb, LANE), lambda bi, c, i: (bi, c, jnp.maximum(i * ratio - 1, 0), 0))
    nxt = pl.BlockSpec((None, None, hb, LANE), lambda bi, c, i: (bi, c, jnp.minimum((i + 1) * ratio, nhb - 1), 0))
    out = jax.ShapeDtypeStruct((b, dil, length, LANE), F32)
    return pl.pallas_call(
        functools.partial(_dil_kernel, length, half),
        out_shape=(out, out),
        grid=(b, dil, length // qb),
        in_specs=[cur, prev, cur, nxt, prev, cur, nxt],
        out_specs=(cur, cur),
        compiler_params=_cparams(("parallel", "parallel", "parallel")),
        name=f"dilated_attention_g{g}",
    )(q, k, k, k, v, v, v)


GDN_TT = 512
N_CHAIN = 2 * H_GDN
W_CHAIN = N_CHAIN * HEAD_DIM
CHAIN_TILE = 256


def _head_ones(width):
    idx = np.arange(width) // HEAD_DIM
    return jnp.asarray((idx[:, None] == idx[None, :]).astype(np.float32))


def _gdn_prep_kernel(seq_tiles, x_ref, xp_ref, xn_ref, ba_ref, cw_ref, alog_ref, dtb_ref, ones_ref, eb_ref,
                     q_ref, k_ref, v_ref, beta_ref, gc_ref):
    i = pl.program_id(1)
    x = x_ref[...].astype(F32)
    tt = x.shape[0]
    halo = xp_ref.shape[0]
    row = lax.broadcasted_iota(I32, (tt, 1), 0)
    prev = jnp.where(i > 0, xp_ref[...].astype(F32), 0.0)
    nxt = jnp.where(i < seq_tiles - 1, xn_ref[...].astype(F32), 0.0)
    acc = x * cw_ref[CONV_K // 2:CONV_K // 2 + 1, :]
    for d in (1, 2):
        dn = pltpu.roll(x, d, 0)
        for e in range(d):
            dn = jnp.where(row == e, prev[halo - d + e:halo - d + e + 1, :], dn)
        acc = acc + dn * cw_ref[CONV_K // 2 - d:CONV_K // 2 - d + 1, :]
        up = pltpu.roll(x, tt - d, 0)
        for e in range(d):
            up = jnp.where(row == tt - d + e, nxt[e:e + 1, :], up)
        acc = acc + up * cw_ref[CONV_K // 2 + d:CONV_K // 2 + d + 1, :]
    act = acc * _sigmoid(acc)
    q, k, v = act[:, :W_GDN], act[:, W_GDN:2 * W_GDN], act[:, 2 * W_GDN:]
    ones = ones_ref[...]

    def head_sumsq(t):
        return sum(_dot(piece, ones) for piece in _split_bf16(t * t, 2))

    q_ref[...] = q * lax.rsqrt(head_sumsq(q) + NORM_EPS) * (HEAD_DIM ** -0.5)
    k_ref[...] = k * lax.rsqrt(head_sumsq(k) + NORM_EPS)
    v_ref[...] = v
    ba = ba_ref[...]
    beta = _sigmoid(ba)
    z = ba + dtb_ref[...]
    g = -jnp.exp(alog_ref[...]) * (jnp.maximum(z, 0.0) + jnp.log1p(jnp.exp(-jnp.abs(z))))
    lane = lax.broadcasted_iota(I32, (1, LANE), 1)
    bg = jnp.where(lane < N_CHAIN, beta, g)
    eb = eb_ref[...]
    expd = sum(_dot(piece, eb) for piece in _split_bf16(bg, 3))
    beta_ref[...] = expd[:, :W_CHAIN]
    pos = row % GDN_CHUNK
    gc_f = expd[:, W_CHAIN:W_CHAIN + W_GDN]
    gc_b = expd[:, W_CHAIN + W_GDN:]
    sh = 1
    while sh < GDN_CHUNK:
        gc_f = gc_f + jnp.where(pos >= sh, pltpu.roll(gc_f, sh, 0), 0.0)
        gc_b = gc_b + jnp.where(pos < GDN_CHUNK - sh, pltpu.roll(gc_b, tt - sh, 0), 0.0)
        sh *= 2
    gc_ref[...] = jnp.concatenate([gc_f, gc_b], axis=1)


GDN_HALO = 16


def _gdn_prep(proj3, ba3, conv_w, a_log, dt_bias):
    b, s, _ = proj3.shape
    tt = min(GDN_TT, s)
    nt = s // tt
    w3 = 3 * W_GDN
    cw = jnp.zeros((8, w3), F32).at[:CONV_K].set(conv_w)
    alog_row = jnp.zeros((1, LANE), F32).at[0, N_CHAIN:2 * N_CHAIN].set(a_log.reshape(-1))
    dtb_row = jnp.zeros((1, LANE), F32).at[0, N_CHAIN:2 * N_CHAIN].set(dt_bias.reshape(-1))
    eb = np.zeros((LANE, 2 * W_CHAIN), np.float32)
    for c in range(N_CHAIN):
        eb[c, c * HEAD_DIM:(c + 1) * HEAD_DIM] = 1.0
        eb[N_CHAIN + c, W_CHAIN + c * HEAD_DIM:W_CHAIN + (c + 1) * HEAD_DIM] = 1.0
    blk = tt // GDN_HALO
    out_w = lambda w: jax.ShapeDtypeStruct((b, s, w), F32)
    const = lambda a: pl.BlockSpec(a.shape, lambda bi, i: (0,) * a.ndim)
    ones = _head_ones(W_GDN)
    eb = jnp.asarray(eb)
    ospec = lambda w: pl.BlockSpec((None, tt, w), lambda bi, i: (bi, i, 0))
    return pl.pallas_call(
        functools.partial(_gdn_prep_kernel, nt),
        out_shape=(out_w(W_GDN), out_w(W_GDN), out_w(W_GDN), out_w(W_CHAIN), out_w(W_CHAIN)),
        grid=(b, nt),
        in_specs=[pl.BlockSpec((None, tt, w3), lambda bi, i: (bi, i, 0)),
                  pl.BlockSpec((None, GDN_HALO, w3), lambda bi, i: (bi, jnp.maximum(i * blk - 1, 0), 0)),
                  pl.BlockSpec((None, GDN_HALO, w3),
                               lambda bi, i: (bi, jnp.minimum((i + 1) * blk, s // GDN_HALO - 1), 0)),
                  pl.BlockSpec((None, tt, LANE), lambda bi, i: (bi, i, 0)),
                  const(cw), const(alog_row), const(dtb_row), const(ones), const(eb)],
        out_specs=(ospec(W_GDN), ospec(W_GDN), ospec(W_GDN), ospec(W_CHAIN), ospec(W_CHAIN)),
        compiler_params=_cparams(("parallel", "parallel")),
        name="gdn_prep",
    )(proj3, proj3, proj3, ba3, cw, alog_row, dtb_row, ones, eb)


GDN_CS = 4


def _gdn_consts():
    c = GDN_CHUNK
    i = np.arange(c)[:, None]
    lane = np.arange(W_CHAIN)[None, :]
    j = lane % HEAD_DIM
    fwd = lane < W_GDN
    incl = np.where(fwd, j <= i, j >= i)
    strict = np.where(fwd, j < i, j > i)
    eye = (j == i)
    masks = np.stack([incl, strict, eye]).astype(np.float32)
    r = np.arange(CHAIN_TILE)[:, None] // HEAD_DIM
    cidx = np.arange(CHAIN_TILE)[None, :] // HEAD_DIM
    bd = (r == cidx).astype(np.float32)
    return jnp.asarray(masks), jnp.asarray(bd)


def _gdn_kernel(qf_ref, kf_ref, vf_ref, qb_ref, kb_ref, vb_ref, betaf_ref, betab_ref, gcf_ref, gcb_ref,
                masks_ref, bd_ref, of_ref, ob_ref, state_ref):
    @pl.when(pl.program_id(1) == 0)
    def _():
        state_ref[...] = jnp.zeros_like(state_ref)

    c = GDN_CHUNK
    ntile = W_CHAIN // CHAIN_TILE
    incl = masks_ref[0]
    strict = masks_ref[1]
    eye = masks_ref[2]
    bd = bd_ref[...]
    bd16 = bd.astype(BF16)
    reps = CHAIN_TILE // HEAD_DIM

    def bdiag(r_t):
        return jnp.concatenate([r_t.astype(BF16)] * reps, axis=0) * bd16

    def chain_mm(lhs, rhs):
        outs = []
        for t in range(ntile):
            sl = slice(t * CHAIN_TILE, (t + 1) * CHAIN_TILE)
            outs.append(_dot(lhs[:, sl], bdiag(rhs[:, sl])))
        return jnp.concatenate(outs, axis=1)

    for step in range(GDN_CS):
        fs = slice(step * c, (step + 1) * c)
        bs = slice((GDN_CS - 1 - step) * c, (GDN_CS - step) * c)
        cat = lambda a, bb: jnp.concatenate([a, bb], axis=1)
        q = cat(qf_ref[fs, :], qb_ref[bs, :])
        k = cat(kf_ref[fs, :], kb_ref[bs, :])
        v = cat(vf_ref[fs, :], vb_ref[bs, :])
        beta = cat(betaf_ref[fs, :W_GDN], betab_ref[bs, W_GDN:])
        gcol = cat(gcf_ref[fs, :W_GDN], gcb_ref[bs, W_GDN:])
        grow = jnp.sum(gcol * eye, axis=0, keepdims=True)
        glast = cat(gcol[c - 1:c, :W_GDN], gcol[0:1, W_GDN:])
        decay = jnp.where(incl > 0, jnp.exp(jnp.where(incl > 0, gcol - grow, 0.0)), 0.0)
        kb = k * beta
        lhs = jnp.concatenate([kb, q], axis=0)
        sc = []
        for t in range(ntile):
            sl = slice(t * CHAIN_TILE, (t + 1) * CHAIN_TILE)
            sc.append(_dot(lhs[:, sl], bdiag(k[:, sl]), NT))
        sc = jnp.concatenate(sc, axis=1)
        a_mat = sc[:c] * decay * strict
        intra = sc[c:] * decay
        p = eye - a_mat
        x = chain_mm(a_mat, a_mat)
        for it in range(5):
            if it < 4:
                both = chain_mm(jnp.concatenate([p, x], axis=0), x)
                p = p + both[:c]
                x = both[c:]
            else:
                p = p + chain_mm(p, x)
        egc = jnp.exp(gcol)
        w = chain_mm(p, kb * egc)
        u = chain_mm(p, v * beta)
        q_dec = q * egc
        k_dec = k * jnp.exp(glast - gcol)
        g_last = jnp.exp(glast)
        lhs2 = jnp.concatenate([w, q_dec], axis=0)
        ws_qs = jnp.concatenate(
            [_dot(lhs2[:, t * CHAIN_TILE:(t + 1) * CHAIN_TILE], state_ref[t]) for t in range(ntile)], axis=1)
        v_new = u - ws_qs[:c]
        o = ws_qs[c:] + chain_mm(intra, v_new)
        for t in range(ntile):
            sl = slice(t * CHAIN_TILE, (t + 1) * CHAIN_TILE)
            state_ref[t] = state_ref[t] * g_last[:, sl] + _dot(k_dec[:, sl], v_new[:, sl], TN) * bd
        of_ref[fs, :] = o[:, :W_GDN]
        ob_ref[bs, :] = o[:, W_GDN:]


def _gdn_main(qn, kn, vv, beta, gc):
    b, s, _ = qn.shape
    ts = GDN_CS * GDN_CHUNK
    nb = s // ts
    masks, bd = _gdn_consts()
    fwd = lambda w: pl.BlockSpec((None, ts, w), lambda bi, i: (bi, i, 0))
    bwd = lambda w: pl.BlockSpec((None, ts, w), lambda bi, i: (bi, nb - 1 - i, 0))
    const = lambda a: pl.BlockSpec(a.shape, lambda bi, i: (0,) * a.ndim)
    out = jax.ShapeDtypeStruct((b, s, W_GDN), F32)
    return pl.pallas_call(
        _gdn_kernel,
        out_shape=(out, out),
        grid=(b, nb),
        in_specs=[fwd(W_GDN), fwd(W_GDN), fwd(W_GDN), bwd(W_GDN), bwd(W_GDN), bwd(W_GDN),
                  fwd(W_CHAIN), bwd(W_CHAIN), fwd(W_CHAIN), bwd(W_CHAIN), const(masks), const(bd)],
        out_specs=(fwd(W_GDN), bwd(W_GDN)),
        scratch_shapes=[pltpu.VMEM((W_CHAIN // CHAIN_TILE, CHAIN_TILE, CHAIN_TILE), F32)],
        compiler_params=_cparams(("parallel", "arbitrary")),
        name="gdn_scan",
    )(qn, kn, vv, qn, kn, vv, beta, beta, gc, gc, masks, bd)


TILE_ROWS = D_MODEL // LANE


def _store_token_tiles(ref, val):
    t = val.shape[0]
    for j in range(TILE_ROWS):
        ref[pl.ds(j, t, stride=TILE_ROWS), :] = val[:, j * LANE:(j + 1) * LANE]


def _load_token_tiles(ref, t):
    return jnp.concatenate([ref[pl.ds(j, t, stride=TILE_ROWS), :] for j in range(TILE_ROWS)], axis=1)


def _layer_norm(y, g, b):
    mu = jnp.mean(y, axis=-1, keepdims=True)
    d = y - mu
    var = jnp.mean(d * d, axis=-1, keepdims=True)
    return d * lax.rsqrt(var + LN_EPS) * g + b


def _split_bf16(a, terms):
    pieces, rest = [], a
    for _ in range(terms):
        piece = rest.astype(BF16)
        pieces.append(piece)
        rest = rest - piece.astype(F32)
    return pieces


def _merge_kernel(x_ref, gates_ref, na_ref, o0, s0, o1, s1, o2, s2, of_ref, ob_ref, z_ref,
                  wna_ref, wdil_ref, wgdn_ref, wout_ref, normw_ref, ones_ref, g_ref, b_ref, wrh_ref, wrl_ref,
                  x1_ref, xa_ref, aff_ref, cls_ref):
    vals = []
    for slot, ref in enumerate((o0, s0, o1, s1, o2, s2)):
        if len(ref.shape) == 2:
            vals.append(ref[...])
        else:
            dil, rows, _ = ref.shape
            for c in range(dil):
                cls_ref[slot, pl.ds(c, rows, stride=dil), :] = ref[c]
            vals.append(cls_ref[slot])
    lses = vals[1::2]
    mm = jnp.maximum(jnp.maximum(lses[0], lses[1]), lses[2])
    wts = [jnp.exp(l - mm) for l in lses]
    o_dil = (wts[0] * vals[0] + wts[1] * vals[2] + wts[2] * vals[4]) / (wts[0] + wts[1] + wts[2])
    o = of_ref[...] + ob_ref[...]
    ones = ones_ref[...]
    ms = sum(_dot(piece, ones) for piece in _split_bf16(o * o, 2)) * (1.0 / HEAD_DIM)
    z = z_ref[...].astype(F32)
    o_gdn = o * lax.rsqrt(ms + NORM_EPS) * normw_ref[...] * (z * _sigmoid(z))
    br_na = _dot(na_ref[...], wna_ref[...])
    br_dil = _dot(o_dil, wdil_ref[...])
    br_gdn = _dot(o_gdn, wgdn_ref[...])
    gate = _sigmoid(gates_ref[...].astype(F32))
    merged = (gate[:, :D_MODEL] * br_na + gate[:, D_MODEL:2 * D_MODEL] * br_dil
              + gate[:, 2 * D_MODEL:] * br_gdn)
    mix = _dot(merged, wout_ref[...])
    x1 = _layer_norm(DN_ALPHA * x_ref[...] + mix, g_ref[...], b_ref[...])
    _store_token_tiles(x1_ref, x1)
    _store_token_tiles(xa_ref, DN_ALPHA * x1)
    xh, xl = _split_bf16(x1, 2)
    logits = _dot(xh, wrh_ref[...]) + _dot(xl, wrh_ref[...]) + _dot(xh, wrl_ref[...])
    e = jnp.exp(logits - jnp.max(logits, axis=-1, keepdims=True))
    aff_ref[...] = e / jnp.sum(e, axis=-1, keepdims=True)


def _merge(x, proj, o_na, dil, o_f, o_b, b, s, w_br_na, w_br_dil, w_br_gdn, w_out, norm_w, ln_g, ln_b, w_router,
           tm=256):
    n = b * s
    tm = min(tm, s)
    nt = s // tm
    tok = lambda w, c=0: pl.BlockSpec((tm, w), lambda bi, i: (bi * nt + i, c))
    const = lambda a: pl.BlockSpec(a.shape, lambda bi, i: (0,) * a.ndim)
    normw = jnp.tile(norm_w, H_GDN).reshape(1, W_GDN)
    ones = _head_ones(W_GDN)
    wr_hi = w_router.astype(BF16)
    wr_lo = (w_router - wr_hi.astype(F32)).astype(BF16)
    consts = [w_br_na.astype(BF16), w_br_dil.astype(BF16), w_br_gdn.astype(BF16), w_out.astype(BF16),
              normw, ones, ln_g.reshape(1, -1), ln_b.reshape(1, -1), wr_hi, wr_lo]
    dil_flat, dil_specs = [], []
    for (_, d), grp in zip(DIL_PATTERN, dil):
        for t in grp:
            dil_flat.append(t)
            if d == 1:
                dil_specs.append(pl.BlockSpec((None, None, tm, LANE), lambda bi, i: (bi, 0, i, 0)))
            else:
                dil_specs.append(pl.BlockSpec((None, d, tm // d, LANE), lambda bi, i: (bi, 0, i, 0)))
    big = jax.ShapeDtypeStruct((n * TILE_ROWS, LANE), F32)
    tile_spec = pl.BlockSpec((tm * TILE_ROWS, LANE), lambda bi, i: (bi * nt + i, 0))
    return pl.pallas_call(
        _merge_kernel,
        out_shape=(big, big, jax.ShapeDtypeStruct((n, N_EXPERTS), F32)),
        grid=(b, nt),
        in_specs=[tok(D_MODEL), tok(N_BRANCH * D_MODEL, C_GATE // (N_BRANCH * D_MODEL)), tok(W_NA)]
                 + dil_specs + [tok(W_GDN), tok(W_GDN), tok(W_GDN, C_Z // W_GDN)]
                 + [const(a) for a in consts],
        out_specs=(tile_spec, tile_spec, tok(N_EXPERTS)),
        scratch_shapes=[pltpu.VMEM((2 * len(DIL_PATTERN), tm, LANE), F32)],
        compiler_params=_cparams(("parallel", "parallel")),
        name="merge_ln1_router",
    )(x, proj, o_na, *dil_flat, o_f, o_b, proj, *consts)


def _thresh_kernel(cap, aff_ref, tau_ref, cgt_ref):
    bits = pltpu.bitcast(aff_ref[...], I32)
    ne = bits.shape[0]

    def count_ge(t):
        return jnp.sum((bits >= t).astype(I32), axis=1, keepdims=True)

    def body(_, carry):
        lo, hi = carry
        mid = lo + jnp.right_shift(hi - lo, 1)
        ok = count_ge(mid) >= cap
        return jnp.where(ok, mid, lo), jnp.where(ok, hi, mid)

    lo0 = jnp.zeros((ne, 1), I32)
    hi0 = jnp.full((ne, 1), 0x7F800000, I32)
    lo, _ = lax.fori_loop(0, 31, body, (lo0, hi0))
    tau_ref[...] = jnp.broadcast_to(lo, tau_ref.shape)
    cgt_ref[...] = jnp.broadcast_to(jnp.sum((bits > lo).astype(I32), axis=1, keepdims=True), cgt_ref.shape)


def _thresholds(aff_t, cap):
    ne, n = aff_t.shape
    out = jax.ShapeDtypeStruct((ne, LANE), I32)
    return pl.pallas_call(
        functools.partial(_thresh_kernel, cap),
        out_shape=(out, out),
        in_specs=[pl.BlockSpec((ne, n), lambda: (0, 0))],
        out_specs=(pl.BlockSpec((ne, LANE), lambda: (0, 0)),) * 2,
        compiler_params=pltpu.CompilerParams(vmem_limit_bytes=VMEM_LIMIT),
        name="expert_thresholds",
    )(aff_t)


CMP_TC = 256
CMP_WIN = CMP_TC // LANE + 1


def _compact_kernel(aff_ref, tau_ref, take_ref, tri_ref, idx_ref, gate_ref, carry_ref):
    step = pl.program_id(0)

    @pl.when(step == 0)
    def _():
        idx_ref[...] = jnp.zeros_like(idx_ref)
        gate_ref[...] = jnp.zeros_like(gate_ref)
        carry_ref[...] = jnp.zeros_like(carry_ref)

    aff = aff_ref[...]
    tc = aff.shape[0]
    bits = pltpu.bitcast(aff, I32)
    tau = tau_ref[...]
    gt = (bits > tau).astype(F32)
    eq = (bits == tau).astype(F32)
    tri = tri_ref[...]
    carry = carry_ref[...]
    tie_incl = _dot(tri, eq) + carry[1:2, :]
    take = eq * ((tie_incl - eq) < take_ref[...].astype(F32)).astype(F32)
    sel = gt + take
    pos_incl = _dot(tri, sel) + carry[0:1, :]
    pos = pos_incl - sel
    carry_ref[0:1, :] = pos_incl[tc - 1:tc, :]
    carry_ref[1:2, :] = tie_incl[tc - 1:tc, :]
    tok = step * tc + lax.broadcasted_iota(I32, (tc, 1), 0)
    lane = lax.broadcasted_iota(I32, (1, LANE), 1)
    for e in range(N_EXPERTS):
        start = carry[0, e].astype(I32)
        r0 = start // LANE
        rel = pos[:, e:e + 1].astype(I32) - r0 * LANE
        sel_e = sel[:, e:e + 1] > 0
        aff_e = aff[:, e:e + 1]
        for j in range(CMP_WIN):
            hit = (rel == lane + j * LANE) & sel_e
            idx_ref[e, r0 + j] += jnp.sum(jnp.where(hit, tok, 0), axis=0, keepdims=True)
            gate_ref[e, r0 + j] += jnp.sum(jnp.where(hit, aff_e, 0.0), axis=0, keepdims=True)


def _compact(aff, tau_row, take_row, cap):
    n, ne = aff.shape
    tc = min(CMP_TC, n)
    rows = cap // LANE + CMP_WIN
    t = np.arange(tc)
    tri = jnp.asarray((t[None, :] <= t[:, None]).astype(np.float32))
    idx, gate = pl.pallas_call(
        _compact_kernel,
        out_shape=(jax.ShapeDtypeStruct((ne, rows, 1, LANE), I32),
                   jax.ShapeDtypeStruct((ne, rows, 1, LANE), F32)),
        grid=(n // tc,),
        in_specs=[pl.BlockSpec((tc, ne), lambda i: (i, 0)),
                  pl.BlockSpec((1, ne), lambda i: (0, 0)),
                  pl.BlockSpec((1, ne), lambda i: (0, 0)),
                  pl.BlockSpec((tc, tc), lambda i: (0, 0))],
        out_specs=(pl.BlockSpec((ne, rows, 1, LANE), lambda i: (0, 0, 0, 0)),) * 2,
        scratch_shapes=[pltpu.VMEM((2, ne), F32)],
        compiler_params=_cparams(("arbitrary",)),
        name="expert_compaction",
    )(aff, tau_row, take_row, tri)
    idx = idx[:, :cap // LANE].reshape(ne, cap)
    gate = gate[:, :cap // LANE].reshape(ne, cap)
    return idx, gate


MOE_TM = 256


def _moe_kernel(idx_ref, idxn_ref, gate_ref, x_hbm, acc_in, wg_ref, wu_ref, wd_ref, acc_hbm, xbuf, ybuf, sems,
                order_sem):
    del acc_in
    nk = pl.num_programs(1)
    step = pl.program_id(0) * nk + pl.program_id(1)
    last = pl.num_programs(0) * nk - 1
    slot = step % 2
    tm = gate_ref.shape[0]
    sem_x, sem_y, sem_s = 0, 2, 3

    rows = tm * TILE_ROWS

    def token(hbm, ref, r):
        return hbm.at[pl.ds(pl.multiple_of(ref[0, r] * TILE_ROWS, TILE_ROWS), TILE_ROWS), :]

    def tile(r):
        return pl.ds(r * TILE_ROWS, TILE_ROWS)

    def x_row(ref, r, sl):
        return pltpu.make_async_copy(token(x_hbm, ref, r), xbuf.at[sl, tile(r), :], sems.at[sem_x + sl])

    def x_tile_wait(sl):
        pltpu.make_async_copy(x_hbm.at[pl.ds(0, rows), :], xbuf.at[sl], sems.at[sem_x + sl]).wait()

    def scatter_wait():
        pltpu.make_async_copy(ybuf, acc_hbm.at[pl.ds(0, rows), :], sems.at[sem_s]).wait()

    @pl.when(step == 0)
    def _():
        for r in range(tm):
            x_row(idx_ref, r, 0).start()

    x_tile_wait(slot)
    for r in range(tm):
        x_row(idxn_ref, r, 1 - slot).start()
    xb = _load_token_tiles(xbuf.at[slot], tm).astype(BF16)
    hg = jnp.dot(xb, wg_ref[...], preferred_element_type=F32)
    hu = jnp.dot(xb, wu_ref[...], preferred_element_type=F32)
    hid = (hg * _sigmoid(hg) * hu).astype(BF16)

    @pl.when(step > 0)
    def _():
        scatter_wait()

    for r in range(tm):
        pltpu.make_async_copy(token(acc_hbm, idx_ref, r), ybuf.at[tile(r), :], sems.at[sem_y]).start()
    pl.semaphore_signal(order_sem, 1)
    pl.semaphore_wait(order_sem, 1)
    ye = jnp.dot(hid, wd_ref[...], preferred_element_type=F32) * gate_ref[...]
    pltpu.make_async_copy(acc_hbm.at[pl.ds(0, rows), :], ybuf, sems.at[sem_y]).wait()
    _store_token_tiles(ybuf, _load_token_tiles(ybuf, tm) + ye)
    for r in range(tm):
        pltpu.make_async_copy(ybuf.at[tile(r), :], token(acc_hbm, idx_ref, r), sems.at[sem_s]).start()

    @pl.when(step == last)
    def _():
        scatter_wait()
        x_tile_wait(1 - slot)


def _moe(x1, acc, idx, gate, wg, wu, wd):
    d = wg.shape[1]
    ne, cap = idx.shape
    tm = min(MOE_TM, cap)
    nk = cap // tm
    f = wg.shape[2]
    idx3 = idx.reshape(ne, nk, 1, tm)
    gate4 = gate.reshape(ne, nk, tm, 1)

    def nxt(e, k):
        s = jnp.minimum(e * nk + k + 1, ne * nk - 1)
        return (s // nk, s % nk, 0, 0)

    return pl.pallas_call(
        _moe_kernel,
        out_shape=jax.ShapeDtypeStruct(acc.shape, F32),
        grid=(ne, nk),
        in_specs=[pl.BlockSpec((None, None, 1, tm), lambda e, k: (e, k, 0, 0), memory_space=pltpu.SMEM),
                  pl.BlockSpec((None, None, 1, tm), nxt, memory_space=pltpu.SMEM),
                  pl.BlockSpec((None, None, tm, 1), lambda e, k: (e, k, 0, 0)),
                  pl.BlockSpec(memory_space=pl.ANY),
                  pl.BlockSpec(memory_space=pl.ANY),
                  pl.BlockSpec((None, d, f), lambda e, k: (e, 0, 0)),
                  pl.BlockSpec((None, d, f), lambda e, k: (e, 0, 0)),
                  pl.BlockSpec((None, f, d), lambda e, k: (e, 0, 0))],
        out_specs=pl.BlockSpec(memory_space=pl.ANY),
        scratch_shapes=[pltpu.VMEM((2, tm * TILE_ROWS, LANE), F32), pltpu.VMEM((tm * TILE_ROWS, LANE), F32),
                        pltpu.SemaphoreType.DMA((4,)), pltpu.SemaphoreType.REGULAR],
        input_output_aliases={4: 0},
        compiler_params=pltpu.CompilerParams(dimension_semantics=("arbitrary", "arbitrary"),
                                             vmem_limit_bytes=VMEM_LIMIT, has_side_effects=True),
        name="expert_ffn",
    )(idx3, idx3, gate4, x1, acc, wg, wu, wd)


def _ln_kernel(x_ref, g_ref, b_ref, o_ref):
    o_ref[...] = _layer_norm(_load_token_tiles(x_ref, o_ref.shape[0]), g_ref[...], b_ref[...])


def _ln(x, g, b, tm=512):
    n, d = x.shape[0] // TILE_ROWS, D_MODEL
    tm = min(tm, n)
    return pl.pallas_call(
        _ln_kernel,
        out_shape=jax.ShapeDtypeStruct((n, d), F32),
        grid=(n // tm,),
        in_specs=[pl.BlockSpec((tm * TILE_ROWS, LANE), lambda i: (i, 0)), pl.BlockSpec((1, d), lambda i: (0, 0)),
                  pl.BlockSpec((1, d), lambda i: (0, 0))],
        out_specs=pl.BlockSpec((tm, d), lambda i: (i, 0)),
        compiler_params=_cparams(("parallel",)),
        name="layer_norm2",
    )(x, g.reshape(1, d), b.reshape(1, d))


def _token_mixer_parts(x, b, s, w_in_p, bias_tab, conv_w, a_log, dt_bias, rope_tabs):
    n = b * s
    proj, ba = _inproj(x, *w_in_p)
    proj3 = proj.reshape(b, s, C_BA)
    o_na = _na_attention(proj3, bias_tab).reshape(n, W_NA)
    dil = [_dil_group(*qkv, g) for g, qkv in enumerate(_rope(proj3, *rope_tabs))]
    qn, kn, vv, beta, gc = _gdn_prep(proj3, ba.reshape(b, s, LANE), conv_w, a_log, dt_bias)
    o_f, o_b = _gdn_main(qn, kn, vv, beta, gc)
    return proj, o_na, dil, o_f.reshape(n, W_GDN), o_b.reshape(n, W_GDN)


def _expert_choice(x1, xa, aff, wg, wu, wd):
    n = aff.shape[0]
    cap = EC_FACTOR * n // N_EXPERTS
    tau, cgt = _thresholds(aff.T, cap)
    tau_row = tau[:, 0].reshape(1, N_EXPERTS)
    take_row = (cap - cgt[:, 0]).reshape(1, N_EXPERTS)
    idx, gate = _compact(aff, tau_row, take_row, cap)
    return _moe(x1, xa, idx, gate, wg, wu, wd)


def _trunk(x3, params, shared):
    b, s, d = x3.shape
    n = b * s
    x = x3.reshape(n, d)
    rope_tabs = _rope_tables(s)
    for l in range(DEPTH):
        p = {k: v[l] for k, v in params.items()}
        sh = shared[l]
        proj, o_na, dil, o_f, o_b = _token_mixer_parts(x, b, s, sh["w_in"], sh["bias_tab"], p["conv_w"],
                                                       p["a_log"], p["dt_bias"], rope_tabs)
        x1, xa, aff = _merge(x, proj, o_na, dil, o_f, o_b, b, s, p["w_br_na"], p["w_br_dil"], p["w_br_gdn"],
                             p["w_out"], p["gdn_norm_w"], p["ln1_g"], p["ln1_b"], p["w_router"])
        acc = _expert_choice(x1, xa, aff, sh["w_gate"], sh["w_up"], sh["w_down"])
        x = _ln(acc, p["ln2_g"], p["ln2_b"])
    return x.reshape(b, s, d)


def kernel(x_prompt, x_sample, w_in, na_rpb, conv_w, a_log, dt_bias, gdn_norm_w, w_br_na, w_br_dil, w_br_gdn,
           w_out, ln1_g, ln1_b, w_router, w_up, w_gate, w_down, ln2_g, ln2_b):
    params = dict(conv_w=conv_w, a_log=a_log, dt_bias=dt_bias, gdn_norm_w=gdn_norm_w, w_br_na=w_br_na,
                  w_br_dil=w_br_dil, w_br_gdn=w_br_gdn, w_out=w_out, ln1_g=ln1_g, ln1_b=ln1_b,
                  w_router=w_router, ln2_g=ln2_g, ln2_b=ln2_b)
    shared = [dict(w_in=_reorder_w_in(w_in[l]), bias_tab=_na_bias_table(na_rpb[l]),
                   w_gate=w_gate[l].astype(BF16), w_up=w_up[l].astype(BF16), w_down=w_down[l].astype(BF16))
              for l in range(DEPTH)]
    return _trunk(x_prompt, params, shared), _trunk(x_sample, params, shared)
```
